```python
import jax, jax.numpy as jnp
from jax import lax
import numpy as np

D_MODEL = 1024
BATCH = 4
SEQ = 8192
DEPTH = 1

GRID_W = 64
HEAD_DIM = 64
NA_HEADS = 8
NA_WIN_ROWS = 8
NA_WIN_COLS = 16
GQA_HEADS = 8
GQA_KV_HEADS = 2
ROPE_BASE = 10000.0
Q_BLOCK = 128
N_EXPERTS = 16
EC_CAPACITY_FACTOR = 2
EXPERT_FF = 2048
N_BRANCHES = 2
EPS = 1e-6

NA_WIDTH = NA_HEADS * HEAD_DIM
GQA_WIDTH = GQA_HEADS * HEAD_DIM
GQA_KV_WIDTH = GQA_KV_HEADS * HEAD_DIM
IN_WIDTH = 3 * NA_WIDTH + GQA_WIDTH + 2 * GQA_KV_WIDTH + N_BRANCHES * D_MODEL

kernel_name = "hybrid_na_gqa_ec_moe_block"


def rmsnorm(x, g):
    x32 = x.astype(jnp.float32)
    y = x32 * lax.rsqrt(jnp.mean(x32 * x32, axis=-1, keepdims=True) + EPS)
    return y.astype(x.dtype) * g


def neighbourhood_attention(q, k, v, rpb):
    B, T, H, hd = q.shape
    rows = T // GRID_W
    kh = min(NA_WIN_ROWS, rows)
    kw = NA_WIN_COLS
    qg = q.reshape(B, rows, GRID_W, H, hd)
    kg = k.reshape(B, rows, GRID_W, H, hd)
    vg = v.reshape(B, rows, GRID_W, H, hd)
    cols = jnp.arange(GRID_W)
    col_start = jnp.clip(cols - kw // 2, 0, GRID_W - kw)
    col_idx = col_start[:, None] + jnp.arange(kw)[None, :]
    dc = col_idx - cols[:, None] + (NA_WIN_COLS - 1)
    scale = hd ** -0.5

    def one_row(r):
        r_start = jnp.clip(r - kh // 2, 0, rows - kh)
        k_band = lax.dynamic_slice_in_dim(kg, r_start, kh, axis=1)
        v_band = lax.dynamic_slice_in_dim(vg, r_start, kh, axis=1)
        k_nb = k_band[:, :, col_idx]
        v_nb = v_band[:, :, col_idx]
        q_r = lax.dynamic_index_in_dim(qg, r, axis=1, keepdims=False)
        dr = r_start + jnp.arange(kh) - r + (NA_WIN_ROWS - 1)
        bias = rpb[:, dr][:, :, dc]
        bias = bias.transpose(0, 2, 1, 3).astype(jnp.float32)
        s = jnp.einsum('bqhd,brqjhd->bhqrj', q_r, k_nb).astype(jnp.float32) * scale + bias[None]
        p = jax.nn.softmax(s.reshape(B, H, GRID_W, kh * kw), axis=-1)
        p = p.reshape(B, H, GRID_W, kh, kw).astype(v.dtype)
        return jnp.einsum('bhqrj,brqjhd->bqhd', p, v_nb)

    out = lax.map(one_row, jnp.arange(rows))
    return out.transpose(1, 0, 2, 3, 4).reshape(B, T, H * hd)


def axial_rope_tables(T):
    t = jnp.arange(T)
    row = (t // GRID_W).astype(jnp.float32)
    col = (t % GRID_W).astype(jnp.float32)
    half = HEAD_DIM // 2
    inv_freq = ROPE_BASE ** (-jnp.arange(0, half, 2, dtype=jnp.float32) / half)
    ang = jnp.concatenate([row[:, None] * inv_freq[None], col[:, None] * inv_freq[None]], axis=-1)
    return jnp.cos(ang), jnp.sin(ang)


def apply_rope(x, cos, sin):
    xp = x.astype(jnp.float32).reshape(*x.shape[:-1], HEAD_DIM // 2, 2)
    x0, x1 = xp[..., 0], xp[..., 1]
    c = cos[None, :, None, :]
    s = sin[None, :, None, :]
    out = jnp.stack([x0 * c - x1 * s, x0 * s + x1 * c], axis=-1).reshape(x.shape)
    return out.astype(x.dtype)


def gqa_attention(q, k, v):
    B, T, Hq, hd = q.shape
    G = Hq // GQA_KV_HEADS
    nblk = T // Q_BLOCK
    qb = q.reshape(B, nblk, Q_BLOCK, GQA_KV_HEADS, G, hd).transpose(1, 0, 2, 3, 4, 5)
    scale = hd ** -0.5

    def one_block(q_blk):
        s = jnp.einsum('bqkgd,bskd->bkgqs', q_blk, k).astype(jnp.float32) * scale
        p = jax.nn.softmax(s, axis=-1).astype(v.dtype)
        return jnp.einsum('bkgqs,bskd->bqkgd', p, v)

    out = lax.map(one_block, qb)
    return out.transpose(1, 0, 2, 3, 4, 5).reshape(B, T, Hq * hd)


def expert_choice_ffn(h, w_router, w_gate, w_up, w_down):
    B, T, D = h.shape
    cap = EC_CAPACITY_FACTOR * T // N_EXPERTS
    logits = jnp.einsum('btd,de->bte', h, w_router).astype(jnp.float32)
    aff = jax.nn.softmax(logits, axis=-1)
    g, idx = lax.top_k(aff.transpose(0, 2, 1), cap)
    xin = jax.vmap(lambda hb, ib: hb[ib])(h, idx)
    a = jnp.einsum('becd,edf->becf', xin, w_gate)
    u = jnp.einsum('becd,edf->becf', xin, w_up)
    y = jnp.einsum('becf,efd->becd', jax.nn.silu(a) * u, w_down)
    y = y * g[..., None].astype(y.dtype)
    flat = (jnp.arange(B)[:, None, None] * T + idx).reshape(-1)
    out = jax.ops.segment_sum(y.reshape(-1, D), flat, num_segments=B * T)
    return out.reshape(B, T, D).astype(h.dtype)


def setup_inputs(seed: int = 0) -> dict:
    key = jax.random.key(seed)
    ks = jax.random.split(key, 20)
    f32 = jnp.float32
    D, L, F, E = D_MODEL, DEPTH, EXPERT_FF, N_EXPERTS
    nrm = lambda k, shape, s: jax.random.normal(k, shape, f32) * s
    return {
        "x": nrm(ks[0], (BATCH, SEQ, D), 1.0),
        "c": nrm(ks[1], (BATCH, D), 1.0),
        "w_ada": nrm(ks[2], (L, D, 6 * D), 0.5 * D ** -0.5),
        "b_ada": nrm(ks[3], (L, 6 * D), 0.02),
        "norm1_g": 1.0 + nrm(ks[4], (L, D), 0.02),
        "w_in": nrm(ks[5], (L, D, IN_WIDTH), D ** -0.5),
        "q_norm_g": 1.0 + nrm(ks[6], (L, HEAD_DIM), 0.02),
        "k_norm_g": 1.0 + nrm(ks[7], (L, HEAD_DIM), 0.02),
        "na_rpb": nrm(ks[8], (L, NA_HEADS, 2 * NA_WIN_ROWS - 1, 2 * NA_WIN_COLS - 1), 0.1),
        "w_branch_na": nrm(ks[9], (L, NA_WIDTH, D), NA_WIDTH ** -0.5),
        "w_branch_gqa": nrm(ks[10], (L, GQA_WIDTH, D), GQA_WIDTH ** -0.5),
        "w_out": nrm(ks[11], (L, D, D), D ** -0.5),
        "norm2_g": 1.0 + nrm(ks[12], (L, D), 0.02),
        "w_router": nrm(ks[13], (L, D, E), D ** -0.5),
        "w_exp_gate": nrm(ks[14], (L, E, D, F), D ** -0.5),
        "w_exp_up": nrm(ks[15], (L, E, D, F), D ** -0.5),
        "w_exp_down": nrm(ks[16], (L, E, F, D), F ** -0.5),
        "final_g": 1.0 + nrm(ks[17], (D,), 0.02),
    }


def reference(x, c, w_ada, b_ada, norm1_g, w_in, q_norm_g, k_norm_g, na_rpb, w_branch_na, w_branch_gqa,
              w_out, norm2_g, w_router, w_exp_gate, w_exp_up, w_exp_down, final_g):
    B, T, D = x.shape
    cos, sin = axial_rope_tables(T)
    split_points = np.cumsum([NA_WIDTH, NA_WIDTH, NA_WIDTH, GQA_WIDTH, GQA_KV_WIDTH, GQA_KV_WIDTH]).tolist()
    for l in range(DEPTH):
        mod = jnp.einsum('bd,de->be', jax.nn.silu(c), w_ada[l]) + b_ada[l]
        shift1, scale1, gate1, shift2, scale2, gate2 = [m[:, None, :] for m in jnp.split(mod, 6, axis=-1)]

        h = rmsnorm(x, norm1_g[l]) * (1.0 + scale1) + shift1
        proj = jnp.einsum('btd,de->bte', h, w_in[l])
        q_na, k_na, v_na, q_g, k_g, v_g, gate_logits = jnp.split(proj, split_points, axis=-1)

        heads = lambda t, n: t.reshape(B, T, n, HEAD_DIM)
        y_na = neighbourhood_attention(heads(q_na, NA_HEADS), heads(k_na, NA_HEADS),
                                       heads(v_na, NA_HEADS), na_rpb[l])

        qg = apply_rope(rmsnorm(heads(q_g, GQA_HEADS), q_norm_g[l]), cos, sin)
        kg = apply_rope(rmsnorm(heads(k_g, GQA_KV_HEADS), k_norm_g[l]), cos, sin)
        y_gqa = gqa_attention(qg, kg, heads(v_g, GQA_KV_HEADS))

        gates = jax.nn.sigmoid(gate_logits.astype(jnp.float32)).astype(x.dtype)
        g_na, g_gqa = jnp.split(gates, N_BRANCHES, axis=-1)
        merged = (g_na * jnp.einsum('btk,kd->btd', y_na, w_branch_na[l])
                  + g_gqa * jnp.einsum('btk,kd->btd', y_gqa, w_branch_gqa[l]))
        x = x + gate1 * jnp.einsum('btd,de->bte', merged, w_out[l])

        h2 = rmsnorm(x, norm2_g[l]) * (1.0 + scale2) + shift2
        x = x + gate2 * expert_choice_ffn(h2, w_router[l], w_exp_gate[l], w_exp_up[l], w_exp_down[l])
    return rmsnorm(x, final_g)
```

```python
import functools

import numpy as np
import jax
import jax.numpy as jnp
from jax import lax
from jax.experimental import pallas as pl
from jax.experimental.pallas import tpu as pltpu

F32 = jnp.float32
BF16 = jnp.bfloat16
I32 = jnp.int32

GRID_W = 64
HEAD_DIM = 64
NA_HEADS = 8
NA_WIN_ROWS = 8
NA_WIN_COLS = 16
GQA_HEADS = 8
GQA_KV_HEADS = 2
ROPE_BASE = 10000.0
N_EXPERTS = 16
EC_CAPACITY_FACTOR = 2
EPS = 1e-6

LANES = 128
SUBLANES = 8
PAIR = LANES // HEAD_DIM
N_PAIRS = NA_HEADS // PAIR
NEG = -1e30
VMEM_LIMIT = 56 * 1024 * 1024

_NT = (((1,), (1,)), ((), ()))


def _dot(a, b, precision=None):
    return jnp.dot(a, b, preferred_element_type=F32, precision=precision)


def _dot_nt(a, b, precision=None):
    return lax.dot_general(a, b, _NT, preferred_element_type=F32, precision=precision)


def _rms(x):
    return x * lax.rsqrt(jnp.mean(x * x, axis=-1, keepdims=True) + EPS)


def _params(*sem):
    return pltpu.CompilerParams(dimension_semantics=sem, vmem_limit_bytes=VMEM_LIMIT)


def _mod_kernel(c_ref, w_ref, b_ref, o_ref):
    c = c_ref[...]
    sc = c * jax.nn.sigmoid(c)
    o_ref[...] = _dot(sc, w_ref[...], lax.Precision.HIGHEST) + b_ref[...]


def _mod(c_pad, w_ada, b_ada):
    rows, d = c_pad.shape
    n = w_ada.shape[1]
    tn = 1024
    return pl.pallas_call(
        _mod_kernel,
        grid=(n // tn,),
        in_specs=[pl.BlockSpec((rows, d), lambda j: (0, 0)),
                  pl.BlockSpec((d, tn), lambda j: (0, j)),
                  pl.BlockSpec((1, tn), lambda j: (0, j))],
        out_specs=pl.BlockSpec((rows, tn), lambda j: (0, j)),
        out_shape=jax.ShapeDtypeStruct((rows, n), F32),
        compiler_params=_params("arbitrary"),
        name="mod",
    )(c_pad, w_ada, b_ada)


def _head_rms_rope(x, gain, cos, sin_signed):
    lane = lax.broadcasted_iota(I32, x.shape, 1)
    low = lane < HEAD_DIM
    ss = x * x
    s_lo = jnp.sum(jnp.where(low, ss, 0.0), axis=-1, keepdims=True)
    s_hi = jnp.sum(jnp.where(low, 0.0, ss), axis=-1, keepdims=True)
    r = jnp.where(low, lax.rsqrt(s_lo / HEAD_DIM + EPS), lax.rsqrt(s_hi / HEAD_DIM + EPS))
    y = (x * r) * gain
    nxt = pltpu.roll(y, LANES - 1, 1)
    prv = pltpu.roll(y, 1, 1)
    partner = jnp.where(lane % 2 == 0, nxt, prv)
    return y * cos + partner * sin_signed


def _proj_kernel(x_ref, shift_ref, scale_ref, g_ref, w_ref, cos_ref, sin_ref, qg_ref, kg_ref,
                 qna_ref, kna_ref, vna_ref, qgq_ref, kgq_ref, vgq_ref):
    x = x_ref[...]
    h = (_rms(x) * g_ref[...]) * (1.0 + scale_ref[0]) + shift_ref[0]
    proj = _dot(h.astype(BF16), w_ref[...])
    scale = HEAD_DIM ** -0.5
    na_w = N_PAIRS * LANES
    for j in range(N_PAIRS):
        c0 = j * LANES
        qna_ref[0, j] = (proj[:, c0:c0 + LANES] * scale).astype(BF16)
        kna_ref[0, j] = proj[:, na_w + c0:na_w + c0 + LANES].astype(BF16)
        vna_ref[0, j] = proj[:, 2 * na_w + c0:2 * na_w + c0 + LANES].astype(BF16)
    cos = cos_ref[...]
    sin = sin_ref[...]
    base = 3 * na_w
    for j in range(N_PAIRS):
        c0 = base + j * LANES
        q = _head_rms_rope(proj[:, c0:c0 + LANES], qg_ref[...], cos, sin)
        qgq_ref[0, j] = (q * scale).astype(BF16)
    c0 = base + N_PAIRS * LANES
    kgq_ref[0] = _head_rms_rope(proj[:, c0:c0 + LANES], kg_ref[...], cos, sin).astype(BF16)
    vgq_ref[0] = proj[:, c0 + LANES:c0 + 2 * LANES].astype(BF16)


def _proj(x2, shift1, scale1, g1, w_qkv, cos_t, sin_t, qg, kg, batch, seq, tm):
    d = x2.shape[1]
    tpb = seq // tm
    pair_shape = jax.ShapeDtypeStruct((batch, N_PAIRS, seq, LANES), BF16)
    kv_shape = jax.ShapeDtypeStruct((batch, seq, LANES), BF16)
    pair_spec = pl.BlockSpec((1, N_PAIRS, tm, LANES), lambda i: (i // tpb, 0, i % tpb, 0))
    kv_spec = pl.BlockSpec((1, tm, LANES), lambda i: (i // tpb, i % tpb, 0))
    vec = lambda: pl.BlockSpec((1, 1, d), lambda i: (i // tpb, 0, 0))
    return pl.pallas_call(
        _proj_kernel,
        grid=(batch * tpb,),
        in_specs=[pl.BlockSpec((tm, d), lambda i: (i, 0)),
                  vec(), vec(),
                  pl.BlockSpec((1, d), lambda i: (0, 0)),
                  pl.BlockSpec(w_qkv.shape, lambda i: (0, 0)),
                  pl.BlockSpec((tm, LANES), lambda i: (i % tpb, 0)),
                  pl.BlockSpec((tm, LANES), lambda i: (i % tpb, 0)),
                  pl.BlockSpec((1, LANES), lambda i: (0, 0)),
                  pl.BlockSpec((1, LANES), lambda i: (0, 0))],
        out_specs=[pair_spec, pair_spec, pair_spec, pair_spec, kv_spec, kv_spec],
        out_shape=[pair_shape, pair_shape, pair_shape, pair_shape, kv_shape, kv_shape],
        compiler_params=_params("arbitrary"),
        name="proj",
    )(x2, shift1, scale1, g1, w_qkv, cos_t, sin_t, qg, kg)


def _na_kernel(q_ref, k_ref, v_ref, bias_ref, o_ref, *, rows):
    band = NA_WIN_ROWS * GRID_W
    lane = lax.broadcasted_iota(I32, (GRID_W, LANES), 1)
    low = lane < HEAD_DIM

    def body(r, carry):
        rs = jnp.clip(r - NA_WIN_ROWS // 2, 0, rows - NA_WIN_ROWS)
        lo = rs - r + (NA_WIN_ROWS - 1)
        q = q_ref[0, 0, pl.ds(pl.multiple_of(r * GRID_W, GRID_W), GRID_W), :]
        kb = k_ref[0, 0, pl.ds(pl.multiple_of(rs * GRID_W, GRID_W), band), :]
        vb = v_ref[0, 0, pl.ds(pl.multiple_of(rs * GRID_W, GRID_W), band), :]
        zero = jnp.zeros_like(q)
        q2 = jnp.concatenate([jnp.where(low, q, zero), jnp.where(low, zero, q)], axis=0)
        s = _dot_nt(q2, kb) + bias_ref[0, lo]
        m = jnp.max(s, axis=-1, keepdims=True)
        p = jnp.exp(s - m)
        l = jnp.sum(p, axis=-1, keepdims=True)
        o2 = _dot(p.astype(BF16), vb) / l
        o = jnp.where(low, o2[:GRID_W], o2[GRID_W:])
        o_ref[0, 0, pl.ds(pl.multiple_of(r * GRID_W, GRID_W), GRID_W), :] = o.astype(o_ref.dtype)
        return carry

    lax.fori_loop(0, rows, body, 0)


def _na(q, k, v, bias):
    batch, n_pairs, seq, _ = q.shape
    rows = seq // GRID_W
    spec = pl.BlockSpec((1, 1, seq, LANES), lambda b, j: (b, j, 0, 0))
    return pl.pallas_call(
        functools.partial(_na_kernel, rows=rows),
        grid=(batch, n_pairs),
        in_specs=[spec, spec, spec,
                  pl.BlockSpec((1,) + bias.shape[1:], lambda b, j: (j, 0, 0, 0))],
        out_specs=spec,
        out_shape=jax.ShapeDtypeStruct(q.shape, BF16),
        compiler_params=_params("arbitrary", "arbitrary"),
        name="na",
    )(q, k, v, bias)


def _na_bias_table(rpb):
    cols = np.arange(GRID_W)
    col_start = np.clip(cols - NA_WIN_COLS // 2, 0, GRID_W - NA_WIN_COLS)
    kc = np.arange(GRID_W)
    valid = (kc[None, :] >= col_start[:, None]) & (kc[None, :] < col_start[:, None] + NA_WIN_COLS)
    dc = np.clip(kc[None, :] - cols[:, None] + (NA_WIN_COLS - 1), 0, 2 * NA_WIN_COLS - 2)
    lo = np.arange(NA_WIN_ROWS)
    jb = np.arange(NA_WIN_ROWS)
    dr = lo[:, None] + jb[None, :]
    t = rpb[:, dr]
    t = t[:, :, :, dc]
    t = jnp.where(jnp.asarray(valid)[None, None, None], t, NEG)
    t = t.transpose(0, 1, 3, 2, 4).reshape(NA_HEADS, NA_WIN_ROWS, GRID_W, NA_WIN_ROWS * GRID_W)
    t = t.reshape(N_PAIRS, PAIR, NA_WIN_ROWS, GRID_W, NA_WIN_ROWS * GRID_W).transpose(0, 2, 1, 3, 4)
    return t.reshape(N_PAIRS, NA_WIN_ROWS, PAIR * GRID_W, NA_WIN_ROWS * GRID_W).astype(F32)


def _gqa_kernel(q_ref, k_ref, v_ref, o_ref):
    k = k_ref[0]
    v = v_ref[0]
    tq = q_ref.shape[2]
    lane = lax.broadcasted_iota(I32, (tq, LANES), 1)
    low = lane < HEAD_DIM
    for j in range(N_PAIRS):
        q = q_ref[0, j]
        zero = jnp.zeros_like(q)
        q2 = jnp.concatenate([jnp.where(low, q, zero), jnp.where(low, zero, q)], axis=0)
        s = _dot_nt(q2, k)
        m = jnp.max(s, axis=-1, keepdims=True)
        p = jnp.exp(s - m)
        l = jnp.sum(p, axis=-1, keepdims=True)
        o2 = _dot(p.astype(BF16), v) / l
        o_ref[0, j] = jnp.where(low, o2[:tq], o2[tq:]).astype(o_ref.dtype)


def _gqa(q, k, v, tq):
    batch, n_pairs, seq, _ = q.shape
    qspec = pl.BlockSpec((1, n_pairs, tq, LANES), lambda b, i: (b, 0, i, 0))
    kvspec = pl.BlockSpec((1, seq, LANES), lambda b, i: (b, 0, 0))
    return pl.pallas_call(
        _gqa_kernel,
        grid=(batch, seq // tq),
        in_specs=[qspec, kvspec, kvspec],
        out_specs=qspec,
        out_shape=jax.ShapeDtypeStruct(q.shape, BF16),
        compiler_params=_params("arbitrary", "arbitrary"),
        name="gqa",
    )(q, k, v)


def _merge_kernel(x_ref, shift1_ref, scale1_ref, gate1_ref, shift2_ref, scale2_ref, g1_ref, g2_ref,
                  wgate_ref, yna_ref, ygq_ref, wna_ref, wgq_ref, wout_ref, wr_ref,
                  x1_ref, h2_ref, aff_ref):
    x = x_ref[...]
    d = x.shape[1]
    h = ((_rms(x) * g1_ref[...]) * (1.0 + scale1_ref[0]) + shift1_ref[0]).astype(BF16)
    gates = jax.nn.sigmoid(_dot(h, wgate_ref[...]))
    yna = jnp.concatenate([yna_ref[0, j] for j in range(N_PAIRS)], axis=-1)
    ygq = jnp.concatenate([ygq_ref[0, j] for j in range(N_PAIRS)], axis=-1)
    merged = gates[:, :d] * _dot(yna, wna_ref[...]) + gates[:, d:] * _dot(ygq, wgq_ref[...])
    x1 = x + gate1_ref[0] * _dot(merged.astype(BF16), wout_ref[...])
    x1_ref[...] = x1
    h2 = (_rms(x1) * g2_ref[...]) * (1.0 + scale2_ref[0]) + shift2_ref[0]
    h2_ref[...] = h2
    logits = _dot_nt(wr_ref[...], h2, lax.Precision.HIGHEST)
    z = jnp.exp(logits - jnp.max(logits, axis=0, keepdims=True))
    aff_ref[0] = z / jnp.sum(z, axis=0, keepdims=True)


def _merge(x2, mods, g1, g2, w_gate, yna, ygq, w_na, w_gq, w_out, w_router_t, batch, seq, tm):
    d = x2.shape[1]
    tpb = seq // tm
    n_exp = w_router_t.shape[0]
    vec = lambda: pl.BlockSpec((1, 1, d), lambda i: (i // tpb, 0, 0))
    full = lambda a: pl.BlockSpec(a.shape, lambda i: (0,) * a.ndim)
    row = pl.BlockSpec((tm, d), lambda i: (i, 0))
    pair_spec = pl.BlockSpec((1, N_PAIRS, tm, LANES), lambda i: (i // tpb, 0, i % tpb, 0))
    return pl.pallas_call(
        _merge_kernel,
        grid=(batch * tpb,),
        in_specs=[row, vec(), vec(), vec(), vec(), vec(), full(g1), full(g2), full(w_gate),
                  pair_spec, pair_spec, full(w_na), full(w_gq), full(w_out), full(w_router_t)],
        out_specs=[row, row, pl.BlockSpec((1, n_exp, tm), lambda i: (i // tpb, 0, i % tpb))],
        out_shape=[jax.ShapeDtypeStruct(x2.shape, F32), jax.ShapeDtypeStruct(x2.shape, F32),
                   jax.ShapeDtypeStruct((batch, n_exp, seq), F32)],
        compiler_params=_params("arbitrary"),
        name="merge",
    )(x2, *mods, g1, g2, w_gate, yna, ygq, w_na, w_gq, w_out, w_router_t)


def _topk_kernel(aff_ref, idx_ref, gate_ref, *, cap):
    n_exp, nblk, _ = aff_ref.shape[1:]
    n_bits = 31
    hi = lax.Precision.HIGHEST

    def search(it, ths):
        bit = jnp.left_shift(jnp.int32(1), n_bits - 1 - it)
        out = []
        for e in range(n_exp):
            bits = pltpu.bitcast(aff_ref[0, e], I32)
            cand = ths[e] | bit
            cnt = jnp.sum((bits >= cand).astype(F32), keepdims=True)
            out.append(jnp.where(cnt >= cap, cand, ths[e]))
        return tuple(out)

    ths = lax.fori_loop(0, n_bits, search, tuple(jnp.zeros((1, 1), I32) for _ in range(n_exp)))

    r_i = lax.broadcasted_iota(I32, (LANES, LANES), 0)
    c_i = lax.broadcasted_iota(I32, (LANES, LANES), 1)
    incl = (r_i <= c_i).astype(BF16)
    br = lax.broadcasted_iota(I32, (nblk, nblk), 0)
    bc = lax.broadcasted_iota(I32, (nblk, nblk), 1)
    strict_lower = (bc < br).astype(BF16)
    incl_blk = (br <= bc).astype(BF16)
    ones_rows = jnp.ones((SUBLANES, LANES), BF16)
    lane_vals = lax.broadcasted_iota(I32, (SUBLANES, LANES), 1).astype(BF16)
    blk_vals = lax.broadcasted_iota(I32, (SUBLANES, nblk), 1).astype(BF16)
    j_col = lax.broadcasted_iota(I32, (cap, nblk), 0).astype(F32)
    j_col_l = lax.broadcasted_iota(I32, (cap, LANES), 0).astype(F32)

    for e in range(n_exp):
        aff = aff_ref[0, e]
        bits = pltpu.bitcast(aff, I32)
        th = ths[e]
        gt = bits > th
        eq = bits == th
        need = cap - jnp.sum(gt.astype(F32), keepdims=True)
        eq_b = eq.astype(BF16)
        eq_cs = _dot(eq_b, incl)
        eq_tot = jnp.broadcast_to(eq_cs[:, LANES - 1:], (nblk, LANES)).astype(BF16)
        eq_rank = _dot(strict_lower, eq_tot) + eq_cs
        sel = gt | (eq & (eq_rank <= need))
        sel_b = sel.astype(BF16)
        cs = jnp.where(sel, _dot(sel_b, incl), 0.0)
        tot_row = _dot_nt(ones_rows, sel_b)
        cum_inc = _dot(tot_row.astype(BF16), incl_blk)[0:1]
        cum_exc = cum_inc - tot_row[0:1]
        onehot = (cum_exc <= j_col) & (j_col < cum_inc)
        onehot_b = onehot.astype(BF16)
        base = jnp.sum(jnp.where(onehot, cum_exc, 0.0), axis=-1, keepdims=True)
        local = j_col_l - base + 1.0
        g_cs = _dot(onehot_b, cs.astype(BF16))
        match = g_cs == local
        g_aff = _dot(onehot.astype(F32), aff, hi)
        blk_row = _dot_nt(blk_vals, onehot_b)
        lane_row = _dot_nt(lane_vals, match.astype(BF16))
        gate_row = _dot_nt(jnp.ones((SUBLANES, LANES), F32), jnp.where(match, g_aff, 0.0), hi)
        idx_ref[0, e:e + 1, :] = (blk_row[0:1] * LANES + lane_row[0:1]).astype(I32)
        gate_ref[0, e:e + 1, :] = gate_row[0:1]


def _topk(aff4, cap):
    batch, n_exp, nblk, _ = aff4.shape
    out_spec = pl.BlockSpec((1, n_exp, cap), lambda b: (b, 0, 0))
    return pl.pallas_call(
        functools.partial(_topk_kernel, cap=cap),
        grid=(batch,),
        in_specs=[pl.BlockSpec((1, n_exp, nblk, LANES), lambda b: (b, 0, 0, 0))],
        out_specs=[out_spec, out_spec],
        out_shape=[jax.ShapeDtypeStruct((batch, n_exp, cap), I32),
                   jax.ShapeDtypeStruct((batch, n_exp, cap), F32)],
        compiler_params=_params("arbitrary"),
        name="topk",
    )(aff4)


def _ffn_kernel(idx_ref, idx_next_ref, h2_ref, wg_ref, wu_ref, wd_ref, y_ref, xbuf, sem,
                *, n_exp, seq, row_chunk, ff_chunk):
    s = pl.program_id(0)
    n_steps = pl.num_programs(0)
    cap = xbuf.shape[1]
    ff = wg_ref.shape[2]
    slot = s % 2

    def row_copy(tok, i, slot_):
        return pltpu.make_async_copy(h2_ref.at[pl.ds(tok, 1), :], xbuf.at[slot_, pl.ds(i, 1), :], sem.at[slot_])

    def gather(ref, step, slot_):
        base = (step // n_exp) * seq

        def body(i, carry):
            row_copy(base + ref[0, 0, i], i, slot_).start()
            return carry

        lax.fori_loop(0, cap, body, 0, unroll=8)

    @pl.when(s == 0)
    def _():
        gather(idx_ref, s, slot)

    @pl.when(s + 1 < n_steps)
    def _():
        gather(idx_next_ref, s + 1, 1 - slot)

    pltpu.make_async_copy(h2_ref.at[pl.ds(0, cap), :], xbuf.at[slot], sem.at[slot]).wait()

    for r0 in range(0, cap, row_chunk):
        x = xbuf[slot, r0:r0 + row_chunk, :].astype(BF16)
        y = None
        for f0 in range(0, ff, ff_chunk):
            a = _dot(x, wg_ref[0, :, f0:f0 + ff_chunk])
            u = _dot(x, wu_ref[0, :, f0:f0 + ff_chunk])
            act = ((a * jax.nn.sigmoid(a)) * u).astype(BF16)
            part = _dot(act, wd_ref[0, f0:f0 + ff_chunk, :])
            y = part if y is None else y + part
        y_ref[0, r0:r0 + row_chunk, :] = y


def _ffn(idx3, h2, wg, wu, wd, batch, seq):
    n_exp, d, ff = wg.shape
    cap = idx3.shape[2]
    n_steps = batch * n_exp
    kern = functools.partial(_ffn_kernel, n_exp=n_exp, seq=seq, row_chunk=min(512, cap), ff_chunk=min(1024, ff))
    smem = lambda f: pl.BlockSpec((1, 1, cap), f, memory_space=pltpu.SMEM)
    return pl.pallas_call(
        kern,
        grid=(n_steps,),
        in_specs=[smem(lambda s: (s, 0, 0)),
                  smem(lambda s: (jnp.minimum(s + 1, n_steps - 1), 0, 0)),
                  pl.BlockSpec(memory_space=pl.ANY),
                  pl.BlockSpec((1, d, ff), lambda s: (s % n_exp, 0, 0)),
                  pl.BlockSpec((1, d, ff), lambda s: (s % n_exp, 0, 0)),
                  pl.BlockSpec((1, ff, d), lambda s: (s % n_exp, 0, 0))],
        out_specs=pl.BlockSpec((1, cap, d), lambda s: (s, 0, 0)),
        out_shape=jax.ShapeDtypeStruct((n_steps, cap, d), F32),
        scratch_shapes=[pltpu.VMEM((2, cap, d), F32), pltpu.SemaphoreType.DMA((2,))],
        compiler_params=_params("arbitrary"),
        name="ffn",
    )(idx3, idx3, h2, wg, wu, wd)


def _combine_kernel(idx_ref, gate_ref, y_ref, gate2_ref, x1_ref, out_ref, acc, sem, *, n_exp, tc):
    b = pl.program_id(0)
    e = pl.program_id(1)
    c = pl.program_id(2)
    n_c = pl.num_programs(2)

    @pl.when((e == 0) & (c == 0))
    def _():
        cp = pltpu.make_async_copy(x1_ref.at[b], acc, sem.at[0])
        cp.start()
        cp.wait()

    gate2 = gate2_ref[0]
    sub = lax.broadcasted_iota(I32, (SUBLANES, acc.shape[1]), 0)

    def group(gi, carry):
        r0 = pl.multiple_of(gi * SUBLANES, SUBLANES)
        rows = y_ref[0, pl.ds(r0, SUBLANES), :] * gate2
        for r in range(SUBLANES):
            j = c * tc + r0 + r
            tok = idx_ref[0, 0, j]
            g = lax.bitcast_convert_type(gate_ref[0, 0, j], F32)
            base = pl.multiple_of((tok >> 3) << 3, SUBLANES)
            upd = jnp.where(sub == (tok & 7), rows[r:r + 1, :] * g, 0.0)
            acc[pl.ds(base, SUBLANES), :] = acc[pl.ds(base, SUBLANES), :] + upd
        return carry

    lax.fori_loop(0, tc // SUBLANES, group, 0)

    @pl.when((e == n_exp - 1) & (c == n_c - 1))
    def _():
        cp = pltpu.make_async_copy(acc, out_ref.at[b], sem.at[1])
        cp.start()
        cp.wait()


def _combine(idx3, gbits3, y, gate2, x1, batch, seq, tc):
    n_steps, cap, d = y.shape
    n_exp = n_steps // batch
    smem = lambda: pl.BlockSpec((1, 1, cap), lambda b, e, c: (b * n_exp + e, 0, 0), memory_space=pltpu.SMEM)
    return pl.pallas_call(
        functools.partial(_combine_kernel, n_exp=n_exp, tc=tc),
        grid=(batch, n_exp, cap // tc),
        in_specs=[smem(), smem(),
                  pl.BlockSpec((1, tc, d), lambda b, e, c: (b * n_exp + e, c, 0)),
                  pl.BlockSpec((1, 1, d), lambda b, e, c: (b, 0, 0)),
                  pl.BlockSpec(memory_space=pl.ANY)],
        out_specs=pl.BlockSpec(memory_space=pl.ANY),
        out_shape=jax.ShapeDtypeStruct((batch, seq, d), F32),
        scratch_shapes=[pltpu.VMEM((seq, d), F32), pltpu.SemaphoreType.DMA((2,))],
        compiler_params=_params("arbitrary", "arbitrary", "arbitrary"),
        name="combine",
    )(idx3, gbits3, y, gate2, x1)


def _final_kernel(x_ref, g_ref, o_ref):
    o_ref[...] = _rms(x_ref[...]) * g_ref[...]


def _final(x2, g, tm):
    n, d = x2.shape
    return pl.pallas_call(
        _final_kernel,
        grid=(n // tm,),
        in_specs=[pl.BlockSpec((tm, d), lambda i: (i, 0)), pl.BlockSpec((1, d), lambda i: (0, 0))],
        out_specs=pl.BlockSpec((tm, d), lambda i: (i, 0)),
        out_shape=jax.ShapeDtypeStruct((n, d), F32),
        compiler_params=_params("arbitrary"),
        name="final",
    )(x2, g)


def _rope_tables(seq):
    t = jnp.arange(seq)
    row = (t // GRID_W).astype(F32)
    col = (t % GRID_W).astype(F32)
    half = HEAD_DIM // 2
    inv_freq = ROPE_BASE ** (-jnp.arange(0, half, 2, dtype=F32) / half)
    ang = jnp.concatenate([row[:, None] * inv_freq[None], col[:, None] * inv_freq[None]], axis=-1)
    cos = jnp.repeat(jnp.cos(ang), 2, axis=-1)
    sin = jnp.repeat(jnp.sin(ang), 2, axis=-1) * jnp.tile(jnp.array([-1.0, 1.0], F32), half)
    return jnp.tile(cos, (1, PAIR)), jnp.tile(sin, (1, PAIR))


def _gqa_head_order():
    group = GQA_HEADS // GQA_KV_HEADS
    order = []
    for j in range(group):
        order += [j, group + j]
    return np.concatenate([np.arange(h * HEAD_DIM, (h + 1) * HEAD_DIM) for h in order])


def kernel(x, c, w_ada, b_ada, norm1_g, w_in, q_norm_g, k_norm_g, na_rpb, w_branch_na, w_branch_gqa,
           w_out, norm2_g, w_router, w_exp_gate, w_exp_up, w_exp_down, final_g):
    batch, seq, d = x.shape
    depth = w_ada.shape[0]
    n_exp = w_router.shape[2]
    cap = EC_CAPACITY_FACTOR * seq // n_exp
    na_w = NA_HEADS * HEAD_DIM
    gq_w = GQA_HEADS * HEAD_DIM
    kv_w = GQA_KV_HEADS * HEAD_DIM
    assert kv_w == LANES and GQA_HEADS // GQA_KV_HEADS == N_PAIRS and seq % LANES == 0

    cos_t, sin_t = _rope_tables(seq)
    perm = _gqa_head_order()
    x2 = x.reshape(batch * seq, d)
    c_pad = jnp.zeros((SUBLANES, d), F32).at[:batch].set(c)

    for l in range(depth):
        mod = _mod(c_pad, w_ada[l], b_ada[l][None, :])[:batch]
        shift1, scale1, gate1, shift2, scale2, gate2 = [m[:, None, :] for m in jnp.split(mod, 6, axis=-1)]

        w = w_in[l]
        q0 = 3 * na_w
        w_qkv = jnp.concatenate([w[:, :q0], w[:, q0:q0 + gq_w][:, perm], w[:, q0 + gq_w:q0 + gq_w + 2 * kv_w]],
                                axis=1).astype(BF16)
        w_gate = w[:, q0 + gq_w + 2 * kv_w:].astype(BF16)
        qg = jnp.tile(q_norm_g[l], PAIR)[None, :]
        kg = jnp.tile(k_norm_g[l], PAIR)[None, :]
        g1 = norm1_g[l][None, :]
        g2 = norm2_g[l][None, :]

        qna, kna, vna, qgq, kgq, vgq = _proj(x2, shift1, scale1, g1, w_qkv, cos_t, sin_t, qg, kg,
                                             batch, seq, tm=512)
        yna = _na(qna, kna, vna, _na_bias_table(na_rpb[l]))
        ygq = _gqa(qgq, kgq, vgq, tq=128)

        x1, h2, aff = _merge(x2, (shift1, scale1, gate1, shift2, scale2), g1, g2, w_gate, yna, ygq,
                             w_branch_na[l].astype(BF16), w_branch_gqa[l][perm].astype(BF16),
                             w_out[l].astype(BF16), w_router[l].T, batch, seq, tm=256)

        idx, gates = _topk(aff.reshape(batch, n_exp, seq // LANES, LANES), cap)
        idx3 = idx.reshape(batch * n_exp, 1, cap)
        gbits3 = lax.bitcast_convert_type(gates, I32).reshape(batch * n_exp, 1, cap)
        y = _ffn(idx3, h2, w_exp_gate[l].astype(BF16), w_exp_up[l].astype(BF16), w_exp_down[l].astype(BF16),
                 batch, seq)
        x2 = _combine(idx3, gbits3, y, gate2, x1.reshape(batch, seq, d), batch, seq,
                      tc=min(256, cap)).reshape(batch * seq, d)

    return _final(x2, final_g[None, :], tm=1024).reshape(batch, seq, d)
```

```python
import functools

import numpy as np
import jax
import jax.numpy as jnp
from jax import lax
from jax.experimental import pallas as pl
from jax.experimental.pallas import tpu as pltpu

F32 = jnp.float32
BF16 = jnp.bfloat16
I32 = jnp.int32

GRID_W = 64
HEAD_DIM = 64
NA_HEADS = 8
NA_WIN_ROWS = 8
NA_WIN_COLS = 16
GQA_HEADS = 8
GQA_KV_HEADS = 2
ROPE_BASE = 10000.0
N_EXPERTS = 16
EC_CAPACITY_FACTOR = 2
EPS = 1e-6

LANES = 128
SUBLANES = 8
PAIR = LANES // HEAD_DIM
N_PAIRS = NA_HEADS // PAIR
NEG = -1e30
LOG2E = 1.4426950408889634
VMEM_LIMIT = 56 * 1024 * 1024

_NT = (((1,), (1,)), ((), ()))


def _dot(a, b, precision=None):
    return jnp.dot(a, b, preferred_element_type=F32, precision=precision)


def _dot_nt(a, b, precision=None):
    return lax.dot_general(a, b, _NT, preferred_element_type=F32, precision=precision)


def _rms(x):
    return x * lax.rsqrt(jnp.mean(x * x, axis=-1, keepdims=True) + EPS)


def _params(*sem):
    return pltpu.CompilerParams(dimension_semantics=sem, vmem_limit_bytes=VMEM_LIMIT)


def _mod_kernel(c_ref, w_ref, b_ref, o_ref):
    c = c_ref[...]
    sc = c * jax.nn.sigmoid(c)
    o_ref[...] = _dot(sc, w_ref[...], lax.Precision.HIGHEST) + b_ref[...]


def _mod(c_pad, w_ada, b_ada):
    rows, d = c_pad.shape
    n = w_ada.shape[1]
    tn = 1024
    return pl.pallas_call(
        _mod_kernel,
        grid=(n // tn,),
        in_specs=[pl.BlockSpec((rows, d), lambda j: (0, 0)),
                  pl.BlockSpec((d, tn), lambda j: (0, j)),
                  pl.BlockSpec((1, tn), lambda j: (0, j))],
        out_specs=pl.BlockSpec((rows, tn), lambda j: (0, j)),
        out_shape=jax.ShapeDtypeStruct((rows, n), F32),
        compiler_params=_params("arbitrary"),
        name="mod",
    )(c_pad, w_ada, b_ada)


def _head_rms_rope(x, gain, cos, sin_signed):
    lane = lax.broadcasted_iota(I32, x.shape, 1)
    low = lane < HEAD_DIM
    ss = x * x
    s_lo = jnp.sum(jnp.where(low, ss, 0.0), axis=-1, keepdims=True)
    s_hi = jnp.sum(jnp.where(low, 0.0, ss), axis=-1, keepdims=True)
    r = jnp.where(low, lax.rsqrt(s_lo / HEAD_DIM + EPS), lax.rsqrt(s_hi / HEAD_DIM + EPS))
    y = (x * r) * gain
    nxt = pltpu.roll(y, LANES - 1, 1)
    prv = pltpu.roll(y, 1, 1)
    partner = jnp.where(lane % 2 == 0, nxt, prv)
    return y * cos + partner * sin_signed


def _proj_kernel(x_ref, shift_ref, scale_ref, g_ref, w_ref, wvt_ref, cos_ref, sin_ref, qg_ref, kg_ref,
                 qna_ref, kna_ref, vna_ref, qgq_ref, kgq_ref, vgqt_ref):
    x = x_ref[...]
    h = ((_rms(x) * g_ref[...]) * (1.0 + scale_ref[0]) + shift_ref[0]).astype(BF16)
    proj = _dot(h, w_ref[...])
    vgqt_ref[0] = _dot_nt(wvt_ref[...], h).astype(BF16)
    scale = HEAD_DIM ** -0.5
    na_w = N_PAIRS * LANES
    for j in range(N_PAIRS):
        c0 = j * LANES
        qna_ref[0, j] = (proj[:, c0:c0 + LANES] * scale).astype(BF16)
        kna_ref[0, j] = proj[:, na_w + c0:na_w + c0 + LANES].astype(BF16)
        vna_ref[0, j] = proj[:, 2 * na_w + c0:2 * na_w + c0 + LANES].astype(BF16)
    cos = cos_ref[...]
    sin = sin_ref[...]
    base = 3 * na_w
    for j in range(N_PAIRS):
        c0 = base + j * LANES
        q = _head_rms_rope(proj[:, c0:c0 + LANES], qg_ref[...], cos, sin)
        qgq_ref[0, j] = (q * (scale * LOG2E)).astype(BF16)
    c0 = base + N_PAIRS * LANES
    kgq_ref[0] = _head_rms_rope(proj[:, c0:c0 + LANES], kg_ref[...], cos, sin).astype(BF16)


def _proj(x2, shift1, scale1, g1, w_qkv, w_vt, cos_t, sin_t, qg, kg, batch, seq, tm):
    d = x2.shape[1]
    tpb = seq // tm
    pair_shape = jax.ShapeDtypeStruct((batch, N_PAIRS, seq, LANES), BF16)
    pair_spec = pl.BlockSpec((1, N_PAIRS, tm, LANES), lambda i: (i // tpb, 0, i % tpb, 0))
    vec = lambda: pl.BlockSpec((1, 1, d), lambda i: (i // tpb, 0, 0))
    return pl.pallas_call(
        _proj_kernel,
        grid=(batch * tpb,),
        in_specs=[pl.BlockSpec((tm, d), lambda i: (i, 0)),
                  vec(), vec(),
                  pl.BlockSpec((1, d), lambda i: (0, 0)),
                  pl.BlockSpec(w_qkv.shape, lambda i: (0, 0)),
                  pl.BlockSpec(w_vt.shape, lambda i: (0, 0)),
                  pl.BlockSpec((tm, LANES), lambda i: (i % tpb, 0)),
                  pl.BlockSpec((tm, LANES), lambda i: (i % tpb, 0)),
                  pl.BlockSpec((1, LANES), lambda i: (0, 0)),
                  pl.BlockSpec((1, LANES), lambda i: (0, 0))],
        out_specs=[pair_spec, pair_spec, pair_spec, pair_spec,
                   pl.BlockSpec((1, tm, LANES), lambda i: (i // tpb, i % tpb, 0)),
                   pl.BlockSpec((1, LANES, tm), lambda i: (i // tpb, 0, i % tpb))],
        out_shape=[pair_shape, pair_shape, pair_shape, pair_shape,
                   jax.ShapeDtypeStruct((batch, seq, LANES), BF16),
                   jax.ShapeDtypeStruct((batch, LANES, seq), BF16)],
        compiler_params=_params("arbitrary"),
        name="proj",
    )(x2, shift1, scale1, g1, w_qkv, w_vt, cos_t, sin_t, qg, kg)


def _na_kernel(q_ref, k_ref, v_ref, bias_ref, o_ref, *, rows):
    band = NA_WIN_ROWS * GRID_W
    lane = lax.broadcasted_iota(I32, (GRID_W, LANES), 1)
    low = lane < HEAD_DIM

    def body(r, carry):
        rs = jnp.clip(r - NA_WIN_ROWS // 2, 0, rows - NA_WIN_ROWS)
        lo = rs - r + (NA_WIN_ROWS - 1)
        q = q_ref[0, 0, pl.ds(pl.multiple_of(r * GRID_W, GRID_W), GRID_W), :]
        kb = k_ref[0, 0, pl.ds(pl.multiple_of(rs * GRID_W, GRID_W), band), :]
        vb = v_ref[0, 0, pl.ds(pl.multiple_of(rs * GRID_W, GRID_W), band), :]
        zero = jnp.zeros_like(q)
        q2 = jnp.concatenate([jnp.where(low, q, zero), jnp.where(low, zero, q)], axis=0)
        s = _dot_nt(q2, kb) + bias_ref[0, lo]
        m = jnp.max(s, axis=-1, keepdims=True)
        p = jnp.exp(s - m)
        l = jnp.sum(p, axis=-1, keepdims=True)
        o2 = _dot(p.astype(BF16), vb) / l
        o = jnp.where(low, o2[:GRID_W], o2[GRID_W:])
        o_ref[0, 0, pl.ds(pl.multiple_of(r * GRID_W, GRID_W), GRID_W), :] = o.astype(o_ref.dtype)
        return carry

    lax.fori_loop(0, rows, body, 0)


def _na(q, k, v, bias):
    batch, n_pairs, seq, _ = q.shape
    rows = seq // GRID_W
    spec = pl.BlockSpec((1, 1, seq, LANES), lambda b, j: (b, j, 0, 0))
    return pl.pallas_call(
        functools.partial(_na_kernel, rows=rows),
        grid=(batch, n_pairs),
        in_specs=[spec, spec, spec,
                  pl.BlockSpec((1,) + bias.shape[1:], lambda b, j: (j, 0, 0, 0))],
        out_specs=spec,
        out_shape=jax.ShapeDtypeStruct(q.shape, BF16),
        compiler_params=_params("arbitrary", "arbitrary"),
        name="na",
    )(q, k, v, bias)


def _na_bias_table(rpb):
    cols = np.arange(GRID_W)
    col_start = np.clip(cols - NA_WIN_COLS // 2, 0, GRID_W - NA_WIN_COLS)
    kc = np.arange(GRID_W)
    valid = (kc[None, :] >= col_start[:, None]) & (kc[None, :] < col_start[:, None] + NA_WIN_COLS)
    dc = np.clip(kc[None, :] - cols[:, None] + (NA_WIN_COLS - 1), 0, 2 * NA_WIN_COLS - 2)
    lo = np.arange(NA_WIN_ROWS)
    jb = np.arange(NA_WIN_ROWS)
    dr = lo[:, None] + jb[None, :]
    t = rpb[:, dr]
    t = t[:, :, :, dc]
    t = jnp.where(jnp.asarray(valid)[None, None, None], t, NEG)
    t = t.transpose(0, 1, 3, 2, 4).reshape(NA_HEADS, NA_WIN_ROWS, GRID_W, NA_WIN_ROWS * GRID_W)
    t = t.reshape(N_PAIRS, PAIR, NA_WIN_ROWS, GRID_W, NA_WIN_ROWS * GRID_W).transpose(0, 2, 1, 3, 4)
    return t.reshape(N_PAIRS, NA_WIN_ROWS, PAIR * GRID_W, NA_WIN_ROWS * GRID_W).astype(F32)


def _col_reduce(op, x, ways=64):
    rows, n = x.shape
    part = op(x.reshape(rows // ways, ways, n), axis=0)
    return op(part, axis=0, keepdims=True)


def _gqa_kernel(q_ref, k_ref, vt_ref, o_ref, *, kc):
    seq = k_ref.shape[1]
    tq = q_ref.shape[2]
    lane = lax.broadcasted_iota(I32, (tq, LANES), 1)
    low = lane < HEAD_DIM

    def masked_q(j):
        q = q_ref[0, j]
        zero = jnp.zeros_like(q)
        return jnp.concatenate([jnp.where(low, q, zero), jnp.where(low, zero, q)], axis=0)

    def scores(step):
        j, c0 = step
        return _dot_nt(k_ref[0, c0:c0 + kc, :], masked_q(j))

    steps = [(j, c0) for c0 in range(0, seq, kc) for j in range(N_PAIRS)]
    ahead = 2
    pending = [scores(s) for s in steps[:ahead]]
    state = {}
    for i, (j, c0) in enumerate(steps):
        st = pending.pop(0)
        if i + ahead < len(steps):
            pending.append(scores(steps[i + ahead]))
        if c0 == 0:
            state[j] = (jnp.full((1, 2 * tq), NEG, F32), jnp.zeros((1, 2 * tq), F32),
                        jnp.zeros((LANES, 2 * tq), F32))
        m, l, acc = state[j]
        m_new = jnp.maximum(m, _col_reduce(jnp.max, st))
        alpha = jnp.exp2(m - m_new)
        p = jnp.exp2(st - m_new)
        l = alpha * l + _col_reduce(jnp.sum, p)
        acc = alpha * acc + _dot(vt_ref[0, :, c0:c0 + kc], p.astype(BF16))
        state[j] = (m_new, l, acc)
        if c0 + kc == seq:
            ot = acc / l
            o_t = jnp.concatenate([ot[:HEAD_DIM, :tq], ot[HEAD_DIM:, tq:]], axis=0)
            o_ref[0, j] = o_t.T.astype(o_ref.dtype)


def _gqa(q, k, vt, tq, kc):
    batch, n_pairs, seq, _ = q.shape
    qspec = pl.BlockSpec((1, n_pairs, tq, LANES), lambda b, i: (b, 0, i, 0))
    return pl.pallas_call(
        functools.partial(_gqa_kernel, kc=kc),
        grid=(batch, seq // tq),
        in_specs=[qspec,
                  pl.BlockSpec((1, seq, LANES), lambda b, i: (b, 0, 0)),
                  pl.BlockSpec((1, LANES, seq), lambda b, i: (b, 0, 0))],
        out_specs=qspec,
        out_shape=jax.ShapeDtypeStruct(q.shape, BF16),
        compiler_params=_params("arbitrary", "arbitrary"),
        name="gqa",
    )(q, k, vt)


def _merge_kernel(x_ref, shift1_ref, scale1_ref, gate1_ref, shift2_ref, scale2_ref, g1_ref, g2_ref,
                  wgate_ref, yna_ref, ygq_ref, wna_ref, wgq_ref, wout_ref, wr_ref,
                  x1_ref, h2_ref, aff_ref):
    x = x_ref[...]
    d = x.shape[1]
    h = ((_rms(x) * g1_ref[...]) * (1.0 + scale1_ref[0]) + shift1_ref[0]).astype(BF16)
    gates = jax.nn.sigmoid(_dot(h, wgate_ref[...]))
    yna = jnp.concatenate([yna_ref[0, j] for j in range(N_PAIRS)], axis=-1)
    ygq = jnp.concatenate([ygq_ref[0, j] for j in range(N_PAIRS)], axis=-1)
    merged = gates[:, :d] * _dot(yna, wna_ref[...]) + gates[:, d:] * _dot(ygq, wgq_ref[...])
    x1 = x + gate1_ref[0] * _dot(merged.astype(BF16), wout_ref[...])
    x1_ref[...] = x1
    h2 = (_rms(x1) * g2_ref[...]) * (1.0 + scale2_ref[0]) + shift2_ref[0]
    h2_ref[...] = h2
    logits = _dot_nt(wr_ref[...], h2, lax.Precision.HIGHEST)
    z = jnp.exp(logits - jnp.max(logits, axis=0, keepdims=True))
    aff_ref[0] = z / jnp.sum(z, axis=0, keepdims=True)


def _merge(x2, mods, g1, g2, w_gate, yna, ygq, w_na, w_gq, w_out, w_router_t, batch, seq, tm):
    d = x2.shape[1]
    tpb = seq // tm
    n_exp = w_router_t.shape[0]
    vec = lambda: pl.BlockSpec((1, 1, d), lambda i: (i // tpb, 0, 0))
    full = lambda a: pl.BlockSpec(a.shape, lambda i: (0,) * a.ndim)
    row = pl.BlockSpec((tm, d), lambda i: (i, 0))
    pair_spec = pl.BlockSpec((1, N_PAIRS, tm, LANES), lambda i: (i // tpb, 0, i % tpb, 0))
    return pl.pallas_call(
        _merge_kernel,
        grid=(batch * tpb,),
        in_specs=[row, vec(), vec(), vec(), vec(), vec(), full(g1), full(g2), full(w_gate),
                  pair_spec, pair_spec, full(w_na), full(w_gq), full(w_out), full(w_router_t)],
        out_specs=[row, row, pl.BlockSpec((1, n_exp, tm), lambda i: (i // tpb, 0, i % tpb))],
        out_shape=[jax.ShapeDtypeStruct(x2.shape, F32), jax.ShapeDtypeStruct(x2.shape, F32),
                   jax.ShapeDtypeStruct((batch, n_exp, seq), F32)],
        compiler_params=_params("arbitrary"),
        name="merge",
    )(x2, *mods, g1, g2, w_gate, yna, ygq, w_na, w_gq, w_out, w_router_t)


def _topk_kernel(aff_ref, idx_ref, gate_ref, *, cap):
    n_exp, nblk, _ = aff_ref.shape[1:]
    n_bits = 31
    hi = lax.Precision.HIGHEST

    def search(it, ths):
        bit = jnp.left_shift(jnp.int32(1), n_bits - 1 - it)
        out = []
        for e in range(n_exp):
            bits = pltpu.bitcast(aff_ref[0, e], I32)
            cand = ths[e] | bit
            cnt = jnp.sum((bits >= cand).astype(F32), keepdims=True)
            out.append(jnp.where(cnt >= cap, cand, ths[e]))
        return tuple(out)

    ths = lax.fori_loop(0, n_bits, search, tuple(jnp.zeros((1, 1), I32) for _ in range(n_exp)))

    r_i = lax.broadcasted_iota(I32, (LANES, LANES), 0)
    c_i = lax.broadcasted_iota(I32, (LANES, LANES), 1)
    incl = (r_i <= c_i).astype(BF16)
    br = lax.broadcasted_iota(I32, (nblk, nblk), 0)
    bc = lax.broadcasted_iota(I32, (nblk, nblk), 1)
    strict_lower = (bc < br).astype(BF16)
    incl_blk = (br <= bc).astype(BF16)
    ones_rows = jnp.ones((SUBLANES, LANES), BF16)
    lane_vals = lax.broadcasted_iota(I32, (SUBLANES, LANES), 1).astype(BF16)
    blk_vals = lax.broadcasted_iota(I32, (SUBLANES, nblk), 1).astype(BF16)
    j_col = lax.broadcasted_iota(I32, (cap, nblk), 0).astype(F32)
    j_col_l = lax.broadcasted_iota(I32, (cap, LANES), 0).astype(F32)

    for e in range(n_exp):
        aff = aff_ref[0, e]
        bits = pltpu.bitcast(aff, I32)
        th = ths[e]
        gt = bits > th
        eq = bits == th
        need = cap - jnp.sum(gt.astype(F32), keepdims=True)
        eq_b = eq.astype(BF16)
        eq_cs = _dot(eq_b, incl)
        eq_tot = jnp.broadcast_to(eq_cs[:, LANES - 1:], (nblk, LANES)).astype(BF16)
        eq_rank = _dot(strict_lower, eq_tot) + eq_cs
        sel = gt | (eq & (eq_rank <= need))
        sel_b = sel.astype(BF16)
        cs = jnp.where(sel, _dot(sel_b, incl), 0.0)
        tot_row = _dot_nt(ones_rows, sel_b)
        cum_inc = _dot(tot_row.astype(BF16), incl_blk)[0:1]
        cum_exc = cum_inc - tot_row[0:1]
        onehot = (cum_exc <= j_col) & (j_col < cum_inc)
        onehot_b = onehot.astype(BF16)
        base = jnp.sum(jnp.where(onehot, cum_exc, 0.0), axis=-1, keepdims=True)
        local = j_col_l - base + 1.0
        g_cs = _dot(onehot_b, cs.astype(BF16))
        match = g_cs == local
        g_aff = _dot(onehot.astype(F32), aff, hi)
        blk_row = _dot_nt(blk_vals, onehot_b)
        lane_row = _dot_nt(lane_vals, match.astype(BF16))
        gate_row = _dot_nt(jnp.ones((SUBLANES, LANES), F32), jnp.where(match, g_aff, 0.0), hi)
        idx_ref[0, e:e + 1, :] = (blk_row[0:1] * LANES + lane_row[0:1]).astype(I32)
        gate_ref[0, e:e + 1, :] = gate_row[0:1]


def _topk(aff4, cap):
    batch, n_exp, nblk, _ = aff4.shape
    out_spec = pl.BlockSpec((1, n_exp, cap), lambda b: (b, 0, 0))
    return pl.pallas_call(
        functools.partial(_topk_kernel, cap=cap),
        grid=(batch,),
        in_specs=[pl.BlockSpec((1, n_exp, nblk, LANES), lambda b: (b, 0, 0, 0))],
        out_specs=[out_spec, out_spec],
        out_shape=[jax.ShapeDtypeStruct((batch, n_exp, cap), I32),
                   jax.ShapeDtypeStruct((batch, n_exp, cap), F32)],
        compiler_params=_params("arbitrary"),
        name="topk",
    )(aff4)


def _ffn_kernel(idx_ref, idx_next_ref, h2_ref, wg_ref, wu_ref, wd_ref, y_ref, xbuf, sem,
                *, n_exp, seq, row_chunk, ff_chunk):
    s = pl.program_id(0)
    n_steps = pl.num_programs(0)
    cap = xbuf.shape[1]
    ff = wg_ref.shape[2]
    slot = s % 2

    def row_copy(tok, i, slot_):
        return pltpu.make_async_copy(h2_ref.at[pl.ds(tok, 1), :], xbuf.at[slot_, pl.ds(i, 1), :], sem.at[slot_])

    def gather(ref, step, slot_):
        base = (step // n_exp) * seq

        def body(i, carry):
            row_copy(base + ref[0, 0, i], i, slot_).start()
            return carry

        lax.fori_loop(0, cap, body, 0, unroll=8)

    @pl.when(s == 0)
    def _():
        gather(idx_ref, s, slot)

    @pl.when(s + 1 < n_steps)
    def _():
        gather(idx_next_ref, s + 1, 1 - slot)

    pltpu.make_async_copy(h2_ref.at[pl.ds(0, cap), :], xbuf.at[slot], sem.at[slot]).wait()

    for r0 in range(0, cap, row_chunk):
        x = xbuf[slot, r0:r0 + row_chunk, :].astype(BF16)
        y = None
        for f0 in range(0, ff, ff_chunk):
            a = _dot(x, wg_ref[0, :, f0:f0 + ff_chunk])
            u = _dot(x, wu_ref[0, :, f0:f0 + ff_chunk])
            act = ((a * jax.nn.sigmoid(a)) * u).astype(BF16)
            part = _dot(act, wd_ref[0, f0:f0 + ff_chunk, :])
            y = part if y is None else y + part
        y_ref[0, r0:r0 + row_chunk, :] = y


def _ffn(idx3, h2, wg, wu, wd, batch, seq):
    n_exp, d, ff = wg.shape
    cap = idx3.shape[2]
    n_steps = batch * n_exp
    kern = functools.partial(_ffn_kernel, n_exp=n_exp, seq=seq, row_chunk=min(512, cap), ff_chunk=min(1024, ff))
    smem = lambda f: pl.BlockSpec((1, 1, cap), f, memory_space=pltpu.SMEM)
    return pl.pallas_call(
        kern,
        grid=(n_steps,),
        in_specs=[smem(lambda s: (s, 0, 0)),
                  smem(lambda s: (jnp.minimum(s + 1, n_steps - 1), 0, 0)),
                  pl.BlockSpec(memory_space=pl.ANY),
                  pl.BlockSpec((1, d, ff), lambda s: (s % n_exp, 0, 0)),
                  pl.BlockSpec((1, d, ff), lambda s: (s % n_exp, 0, 0)),
                  pl.BlockSpec((1, ff, d), lambda s: (s % n_exp, 0, 0))],
        out_specs=pl.BlockSpec((1, cap, d), lambda s: (s, 0, 0)),
        out_shape=jax.ShapeDtypeStruct((n_steps, cap, d), F32),
        scratch_shapes=[pltpu.VMEM((2, cap, d), F32), pltpu.SemaphoreType.DMA((2,))],
        compiler_params=_params("arbitrary"),
        name="ffn",
    )(idx3, idx3, h2, wg, wu, wd)


def _combine_kernel(idx_ref, gate_ref, y_ref, gate2_ref, x1_ref, out_ref, acc, sem, *, n_exp, tc):
    b = pl.program_id(0)
    e = pl.program_id(1)
    c = pl.program_id(2)
    n_c = pl.num_programs(2)

    @pl.when((e == 0) & (c == 0))
    def _():
        cp = pltpu.make_async_copy(x1_ref.at[b], acc, sem.at[0])
        cp.start()
        cp.wait()

    gate2 = gate2_ref[0]
    sub = lax.broadcasted_iota(I32, (SUBLANES, acc.shape[1]), 0)

    def group(gi, carry):
        r0 = pl.multiple_of(gi * SUBLANES, SUBLANES)
        rows = y_ref[0, pl.ds(r0, SUBLANES), :] * gate2
        for r in range(SUBLANES):
            j = c * tc + r0 + r
            tok = idx_ref[0, 0, j]
            g = lax.bitcast_convert_type(gate_ref[0, 0, j], F32)
            base = pl.multiple_of((tok >> 3) << 3, SUBLANES)
            upd = jnp.where(sub == (tok & 7), rows[r:r + 1, :] * g, 0.0)
            acc[pl.ds(base, SUBLANES), :] = acc[pl.ds(base, SUBLANES), :] + upd
        return carry

    lax.fori_loop(0, tc // SUBLANES, group, 0)

    @pl.when((e == n_exp - 1) & (c == n_c - 1))
    def _():
        cp = pltpu.make_async_copy(acc, out_ref.at[b], sem.at[1])
        cp.start()
        cp.wait()


def _combine(idx3, gbits3, y, gate2, x1, batch, seq, tc):
    n_steps, cap, d = y.shape
    n_exp = n_steps // batch
    smem = lambda: pl.BlockSpec((1, 1, cap), lambda b, e, c: (b * n_exp + e, 0, 0), memory_space=pltpu.SMEM)
    return pl.pallas_call(
        functools.partial(_combine_kernel, n_exp=n_exp, tc=tc),
        grid=(batch, n_exp, cap // tc),
        in_specs=[smem(), smem(),
                  pl.BlockSpec((1, tc, d), lambda b, e, c: (b * n_exp + e, c, 0)),
                  pl.BlockSpec((1, 1, d), lambda b, e, c: (b, 0, 0)),
                  pl.BlockSpec(memory_space=pl.ANY)],
        out_specs=pl.BlockSpec(memory_space=pl.ANY),
        out_shape=jax.ShapeDtypeStruct((batch, seq, d), F32),
        scratch_shapes=[pltpu.VMEM((seq, d), F32), pltpu.SemaphoreType.DMA((2,))],
        compiler_params=_params("arbitrary", "arbitrary", "arbitrary"),
        name="combine",
    )(idx3, gbits3, y, gate2, x1)


def _final_kernel(x_ref, g_ref, o_ref):
    o_ref[...] = _rms(x_ref[...]) * g_ref[...]


def _final(x2, g, tm):
    n, d = x2.shape
    return pl.pallas_call(
        _final_kernel,
        grid=(n // tm,),
        in_specs=[pl.BlockSpec((tm, d), lambda i: (i, 0)), pl.BlockSpec((1, d), lambda i: (0, 0))],
        out_specs=pl.BlockSpec((tm, d), lambda i: (i, 0)),
        out_shape=jax.ShapeDtypeStruct((n, d), F32),
        compiler_params=_params("arbitrary"),
        name="final",
    )(x2, g)


def _rope_tables(seq):
    t = jnp.arange(seq)
    row = (t // GRID_W).astype(F32)
    col = (t % GRID_W).astype(F32)
    half = HEAD_DIM // 2
    inv_freq = ROPE_BASE ** (-jnp.arange(0, half, 2, dtype=F32) / half)
    ang = jnp.concatenate([row[:, None] * inv_freq[None], col[:, None] * inv_freq[None]], axis=-1)
    cos = jnp.repeat(jnp.cos(ang), 2, axis=-1)
    sin = jnp.repeat(jnp.sin(ang), 2, axis=-1) * jnp.tile(jnp.array([-1.0, 1.0], F32), half)
    return jnp.tile(cos, (1, PAIR)), jnp.tile(sin, (1, PAIR))


def _gqa_head_order():
    group = GQA_HEADS // GQA_KV_HEADS
    order = []
    for j in range(group):
        order += [j, group + j]
    return np.concatenate([np.arange(h * HEAD_DIM, (h + 1) * HEAD_DIM) for h in order])


def kernel(x, c, w_ada, b_ada, norm1_g, w_in, q_norm_g, k_norm_g, na_rpb, w_branch_na, w_branch_gqa,
           w_out, norm2_g, w_router, w_exp_gate, w_exp_up, w_exp_down, final_g):
    batch, seq, d = x.shape
    depth = w_ada.shape[0]
    n_exp = w_router.shape[2]
    cap = EC_CAPACITY_FACTOR * seq // n_exp
    na_w = NA_HEADS * HEAD_DIM
    gq_w = GQA_HEADS * HEAD_DIM
    kv_w = GQA_KV_HEADS * HEAD_DIM
    assert kv_w == LANES and GQA_HEADS // GQA_KV_HEADS == N_PAIRS and seq % LANES == 0

    cos_t, sin_t = _rope_tables(seq)
    perm = _gqa_head_order()
    x2 = x.reshape(batch * seq, d)
    c_pad = jnp.zeros((SUBLANES, d), F32).at[:batch].set(c)

    for l in range(depth):
        mod = _mod(c_pad, w_ada[l], b_ada[l][None, :])[:batch]
        shift1, scale1, gate1, shift2, scale2, gate2 = [m[:, None, :] for m in jnp.split(mod, 6, axis=-1)]

        w = w_in[l]
        q0 = 3 * na_w
        w_qkv = jnp.concatenate([w[:, :q0], w[:, q0:q0 + gq_w][:, perm], w[:, q0 + gq_w:q0 + gq_w + kv_w]],
                                axis=1).astype(BF16)
        w_vt = w[:, q0 + gq_w + kv_w:q0 + gq_w + 2 * kv_w].T.astype(BF16)
        w_gate = w[:, q0 + gq_w + 2 * kv_w:].astype(BF16)
        qg = jnp.tile(q_norm_g[l], PAIR)[None, :]
        kg = jnp.tile(k_norm_g[l], PAIR)[None, :]
        g1 = norm1_g[l][None, :]
        g2 = norm2_g[l][None, :]

        qna, kna, vna, qgq, kgq, vgqt = _proj(x2, shift1, scale1, g1, w_qkv, w_vt, cos_t, sin_t, qg, kg,
                                              batch, seq, tm=512)
        yna = _na(qna, kna, vna, _na_bias_table(na_rpb[l]))
        ygq = _gqa(qgq, kgq, vgqt, tq=128, kc=min(1024, seq))

        x1, h2, aff = _merge(x2, (shift1, scale1, gate1, shift2, scale2), g1, g2, w_gate, yna, ygq,
                             w_branch_na[l].astype(BF16), w_branch_gqa[l][perm].astype(BF16),
                             w_out[l].astype(BF16), w_router[l].T, batch, seq, tm=256)

        idx, gates = _topk(aff.reshape(batch, n_exp, seq // LANES, LANES), cap)
        idx3 = idx.reshape(batch * n_exp, 1, cap)
        gbits3 = lax.bitcast_convert_type(gates, I32).reshape(batch * n_exp, 1, cap)
        y = _ffn(idx3, h2, w_exp_gate[l].astype(BF16), w_exp_up[l].astype(BF16), w_exp_down[l].astype(BF16),
                 batch, seq)
        x2 = _combine(idx3, gbits3, y, gate2, x1.reshape(batch, seq, d), batch, seq,
                      tc=min(256, cap)).reshape(batch * seq, d)

    return _final(x2, final_g[None, :], tm=1024).reshape(batch, seq, d)
```

```python
import functools

import numpy as np
import jax
import jax.numpy as jnp
from jax import lax
from jax.experimental import pallas as pl
from jax.experimental.pallas import tpu as pltpu

F32 = jnp.float32
BF16 = jnp.bfloat16
I32 = jnp.int32

GRID_W = 64
HEAD_DIM = 64
NA_HEADS = 8
NA_WIN_ROWS = 8
NA_WIN_COLS = 16
GQA_HEADS = 8
GQA_KV_HEADS = 2
ROPE_BASE = 10000.0
N_EXPERTS = 16
EC_CAPACITY_FACTOR = 2
EPS = 1e-6

LANES = 128
SUBLANES = 8
PAIR = LANES // HEAD_DIM
N_PAIRS = NA_HEADS // PAIR
NEG = -1e30
LOG2E = 1.4426950408889634
VMEM_LIMIT = 56 * 1024 * 1024

_NT = (((1,), (1,)), ((), ()))


def _dot(a, b, precision=None):
    return jnp.dot(a, b, preferred_element_type=F32, precision=precision)


def _dot_nt(a, b, precision=None):
    return lax.dot_general(a, b, _NT, preferred_element_type=F32, precision=precision)


def _rms(x):
    return x * lax.rsqrt(jnp.mean(x * x, axis=-1, keepdims=True) + EPS)


def _params(*sem):
    return pltpu.CompilerParams(dimension_semantics=sem, vmem_limit_bytes=VMEM_LIMIT)


def _mod_kernel(c_ref, w_ref, b_ref, o_ref):
    c = c_ref[...]
    sc = c * jax.nn.sigmoid(c)
    o_ref[...] = _dot(sc, w_ref[...], lax.Precision.HIGHEST) + b_ref[...]


def _mod(c_pad, w_ada, b_ada):
    rows, d = c_pad.shape
    n = w_ada.shape[1]
    tn = 1024
    return pl.pallas_call(
        _mod_kernel,
        grid=(n // tn,),
        in_specs=[pl.BlockSpec((rows, d), lambda j: (0, 0)),
                  pl.BlockSpec((d, tn), lambda j: (0, j)),
                  pl.BlockSpec((1, tn), lambda j: (0, j))],
        out_specs=pl.BlockSpec((rows, tn), lambda j: (0, j)),
        out_shape=jax.ShapeDtypeStruct((rows, n), F32),
        compiler_params=_params("arbitrary"),
        name="mod",
    )(c_pad, w_ada, b_ada)


def _head_rms_rope(x, gain, cos, sin_signed):
    lane = lax.broadcasted_iota(I32, x.shape, 1)
    low = lane < HEAD_DIM
    ss = x * x
    s_lo = jnp.sum(jnp.where(low, ss, 0.0), axis=-1, keepdims=True)
    s_hi = jnp.sum(jnp.where(low, 0.0, ss), axis=-1, keepdims=True)
    r = jnp.where(low, lax.rsqrt(s_lo / HEAD_DIM + EPS), lax.rsqrt(s_hi / HEAD_DIM + EPS))
    y = (x * r) * gain
    nxt = pltpu.roll(y, LANES - 1, 1)
    prv = pltpu.roll(y, 1, 1)
    partner = jnp.where(lane % 2 == 0, nxt, prv)
    return y * cos + partner * sin_signed


def _proj_kernel(x_ref, shift_ref, scale_ref, g_ref, w_ref, wvt_ref, cos_ref, sin_ref, qg_ref, kg_ref,
                 qna_ref, kna_ref, vna_ref, qgq_ref, kgq_ref, vgqt_ref):
    x = x_ref[...]
    h = ((_rms(x) * g_ref[...]) * (1.0 + scale_ref[0]) + shift_ref[0]).astype(BF16)
    proj = _dot(h, w_ref[...])
    vgqt_ref[0] = _dot_nt(wvt_ref[...], h).astype(BF16)
    scale = HEAD_DIM ** -0.5
    na_w = N_PAIRS * LANES
    for j in range(N_PAIRS):
        c0 = j * LANES
        qna_ref[0, j] = (proj[:, c0:c0 + LANES] * scale).astype(BF16)
        kna_ref[0, j] = proj[:, na_w + c0:na_w + c0 + LANES].astype(BF16)
        vna_ref[0, j] = proj[:, 2 * na_w + c0:2 * na_w + c0 + LANES].astype(BF16)
    cos = cos_ref[...]
    sin = sin_ref[...]
    base = 3 * na_w
    for j in range(N_PAIRS):
        c0 = base + j * LANES
        q = _head_rms_rope(proj[:, c0:c0 + LANES], qg_ref[...], cos, sin)
        qgq_ref[0, j] = (q * (scale * LOG2E)).astype(BF16)
    c0 = base + N_PAIRS * LANES
    kgq_ref[0] = _head_rms_rope(proj[:, c0:c0 + LANES], kg_ref[...], cos, sin).astype(BF16)


def _proj(x2, shift1, scale1, g1, w_qkv, w_vt, cos_t, sin_t, qg, kg, batch, seq, tm):
    d = x2.shape[1]
    tpb = seq // tm
    pair_shape = jax.ShapeDtypeStruct((batch, N_PAIRS, seq, LANES), BF16)
    pair_spec = pl.BlockSpec((1, N_PAIRS, tm, LANES), lambda i: (i // tpb, 0, i % tpb, 0))
    vec = lambda: pl.BlockSpec((1, 1, d), lambda i: (i // tpb, 0, 0))
    return pl.pallas_call(
        _proj_kernel,
        grid=(batch * tpb,),
        in_specs=[pl.BlockSpec((tm, d), lambda i: (i, 0)),
                  vec(), vec(),
                  pl.BlockSpec((1, d), lambda i: (0, 0)),
                  pl.BlockSpec(w_qkv.shape, lambda i: (0, 0)),
                  pl.BlockSpec(w_vt.shape, lambda i: (0, 0)),
                  pl.BlockSpec((tm, LANES), lambda i: (i % tpb, 0)),
                  pl.BlockSpec((tm, LANES), lambda i: (i % tpb, 0)),
                  pl.BlockSpec((1, LANES), lambda i: (0, 0)),
                  pl.BlockSpec((1, LANES), lambda i: (0, 0))],
        out_specs=[pair_spec, pair_spec, pair_spec, pair_spec,
                   pl.BlockSpec((1, tm, LANES), lambda i: (i // tpb, i % tpb, 0)),
                   pl.BlockSpec((1, LANES, tm), lambda i: (i // tpb, 0, i % tpb))],
        out_shape=[pair_shape, pair_shape, pair_shape, pair_shape,
                   jax.ShapeDtypeStruct((batch, seq, LANES), BF16),
                   jax.ShapeDtypeStruct((batch, LANES, seq), BF16)],
        compiler_params=_params("arbitrary"),
        name="proj",
    )(x2, shift1, scale1, g1, w_qkv, w_vt, cos_t, sin_t, qg, kg)


def _na_kernel(q_ref, k_ref, v_ref, bias_ref, o_ref, *, rows):
    band = NA_WIN_ROWS * GRID_W
    lane = lax.broadcasted_iota(I32, (GRID_W, LANES), 1)
    low = lane < HEAD_DIM

    def body(r, carry):
        rs = jnp.clip(r - NA_WIN_ROWS // 2, 0, rows - NA_WIN_ROWS)
        lo = rs - r + (NA_WIN_ROWS - 1)
        q = q_ref[0, 0, pl.ds(pl.multiple_of(r * GRID_W, GRID_W), GRID_W), :]
        kb = k_ref[0, 0, pl.ds(pl.multiple_of(rs * GRID_W, GRID_W), band), :]
        vb = v_ref[0, 0, pl.ds(pl.multiple_of(rs * GRID_W, GRID_W), band), :]
        zero = jnp.zeros_like(q)
        q2 = jnp.concatenate([jnp.where(low, q, zero), jnp.where(low, zero, q)], axis=0)
        s = _dot_nt(q2, kb) + bias_ref[0, lo]
        m = jnp.max(s, axis=-1, keepdims=True)
        p = jnp.exp(s - m)
        l = jnp.sum(p, axis=-1, keepdims=True)
        o2 = _dot(p.astype(BF16), vb) / l
        o = jnp.where(low, o2[:GRID_W], o2[GRID_W:])
        o_ref[0, 0, pl.ds(pl.multiple_of(r * GRID_W, GRID_W), GRID_W), :] = o.astype(o_ref.dtype)
        return carry

    lax.fori_loop(0, rows, body, 0, unroll=min(4, rows))


def _na(q, k, v, bias):
    batch, n_pairs, seq, _ = q.shape
    rows = seq // GRID_W
    spec = pl.BlockSpec((1, 1, seq, LANES), lambda b, j: (b, j, 0, 0))
    return pl.pallas_call(
        functools.partial(_na_kernel, rows=rows),
        grid=(batch, n_pairs),
        in_specs=[spec, spec, spec,
                  pl.BlockSpec((1,) + bias.shape[1:], lambda b, j: (j, 0, 0, 0))],
        out_specs=spec,
        out_shape=jax.ShapeDtypeStruct(q.shape, BF16),
        compiler_params=_params("arbitrary", "arbitrary"),
        name="na",
    )(q, k, v, bias)


def _na_bias_table(rpb):
    cols = np.arange(GRID_W)
    col_start = np.clip(cols - NA_WIN_COLS // 2, 0, GRID_W - NA_WIN_COLS)
    kc = np.arange(GRID_W)
    valid = (kc[None, :] >= col_start[:, None]) & (kc[None, :] < col_start[:, None] + NA_WIN_COLS)
    dc = np.clip(kc[None, :] - cols[:, None] + (NA_WIN_COLS - 1), 0, 2 * NA_WIN_COLS - 2)
    lo = np.arange(NA_WIN_ROWS)
    jb = np.arange(NA_WIN_ROWS)
    dr = lo[:, None] + jb[None, :]
    t = rpb[:, dr]
    t = t[:, :, :, dc]
    t = jnp.where(jnp.asarray(valid)[None, None, None], t, NEG)
    t = t.transpose(0, 1, 3, 2, 4).reshape(NA_HEADS, NA_WIN_ROWS, GRID_W, NA_WIN_ROWS * GRID_W)
    t = t.reshape(N_PAIRS, PAIR, NA_WIN_ROWS, GRID_W, NA_WIN_ROWS * GRID_W).transpose(0, 2, 1, 3, 4)
    return t.reshape(N_PAIRS, NA_WIN_ROWS, PAIR * GRID_W, NA_WIN_ROWS * GRID_W).astype(F32)


def _col_reduce(op, x, ways=64):
    rows, n = x.shape
    part = op(x.reshape(rows // ways, ways, n), axis=0)
    return op(part, axis=0, keepdims=True)


def _gqa_kernel(q_ref, k_ref, vt_ref, o_ref, *, kc):
    seq = k_ref.shape[1]
    tq = q_ref.shape[2]
    lane = lax.broadcasted_iota(I32, (tq, LANES), 1)
    low = lane < HEAD_DIM

    def masked_q(j):
        q = q_ref[0, j]
        zero = jnp.zeros_like(q)
        return jnp.concatenate([jnp.where(low, q, zero), jnp.where(low, zero, q)], axis=0)

    def scores(step):
        j, c0 = step
        return _dot_nt(k_ref[0, c0:c0 + kc, :], masked_q(j))

    steps = [(j, c0) for c0 in range(0, seq, kc) for j in range(N_PAIRS)]
    ahead = 2
    pending = [scores(s) for s in steps[:ahead]]
    state = {}
    for i, (j, c0) in enumerate(steps):
        st = pending.pop(0)
        if i + ahead < len(steps):
            pending.append(scores(steps[i + ahead]))
        if c0 == 0:
            state[j] = (jnp.full((1, 2 * tq), NEG, F32), jnp.zeros((1, 2 * tq), F32),
                        jnp.zeros((LANES, 2 * tq), F32))
        m, l, acc = state[j]
        m_new = jnp.maximum(m, _col_reduce(jnp.max, st))
        alpha = jnp.exp2(m - m_new)
        p = jnp.exp2(st - m_new)
        l = alpha * l + _col_reduce(jnp.sum, p)
        acc = alpha * acc + _dot(vt_ref[0, :, c0:c0 + kc], p.astype(BF16))
        state[j] = (m_new, l, acc)
        if c0 + kc == seq:
            ot = acc / l
            o_t = jnp.concatenate([ot[:HEAD_DIM, :tq], ot[HEAD_DIM:, tq:]], axis=0)
            o_ref[0, j] = o_t.T.astype(o_ref.dtype)


def _gqa(q, k, vt, tq, kc):
    batch, n_pairs, seq, _ = q.shape
    qspec = pl.BlockSpec((1, n_pairs, tq, LANES), lambda b, i: (b, 0, i, 0))
    return pl.pallas_call(
        functools.partial(_gqa_kernel, kc=kc),
        grid=(batch, seq // tq),
        in_specs=[qspec,
                  pl.BlockSpec((1, seq, LANES), lambda b, i: (b, 0, 0)),
                  pl.BlockSpec((1, LANES, seq), lambda b, i: (b, 0, 0))],
        out_specs=qspec,
        out_shape=jax.ShapeDtypeStruct(q.shape, BF16),
        compiler_params=_params("arbitrary", "arbitrary"),
        name="gqa",
    )(q, k, vt)


def _merge_kernel(x_ref, shift1_ref, scale1_ref, gate1_ref, shift2_ref, scale2_ref, g1_ref, g2_ref,
                  wgate_ref, yna_ref, ygq_ref, wna_ref, wgq_ref, wout_ref, wr_ref,
                  x1_ref, h2_ref, aff_ref):
    x = x_ref[...]
    d = x.shape[1]
    h = ((_rms(x) * g1_ref[...]) * (1.0 + scale1_ref[0]) + shift1_ref[0]).astype(BF16)
    gates = jax.nn.sigmoid(_dot(h, wgate_ref[...]))
    yna = jnp.concatenate([yna_ref[0, j] for j in range(N_PAIRS)], axis=-1)
    ygq = jnp.concatenate([ygq_ref[0, j] for j in range(N_PAIRS)], axis=-1)
    merged = gates[:, :d] * _dot(yna, wna_ref[...]) + gates[:, d:] * _dot(ygq, wgq_ref[...])
    x1 = x + gate1_ref[0] * _dot(merged.astype(BF16), wout_ref[...])
    x1_ref[...] = x1
    h2 = (_rms(x1) * g2_ref[...]) * (1.0 + scale2_ref[0]) + shift2_ref[0]
    h2_ref[...] = h2
    logits = _dot_nt(wr_ref[...], h2, lax.Precision.HIGHEST)
    z = jnp.exp(logits - jnp.max(logits, axis=0, keepdims=True))
    aff_ref[0] = z / jnp.sum(z, axis=0, keepdims=True)


def _merge(x2, mods, g1, g2, w_gate, yna, ygq, w_na, w_gq, w_out, w_router_t, batch, seq, tm):
    d = x2.shape[1]
    tpb = seq // tm
    n_exp = w_router_t.shape[0]
    vec = lambda: pl.BlockSpec((1, 1, d), lambda i: (i // tpb, 0, 0))
    full = lambda a: pl.BlockSpec(a.shape, lambda i: (0,) * a.ndim)
    row = pl.BlockSpec((tm, d), lambda i: (i, 0))
    pair_spec = pl.BlockSpec((1, N_PAIRS, tm, LANES), lambda i: (i // tpb, 0, i % tpb, 0))
    return pl.pallas_call(
        _merge_kernel,
        grid=(batch * tpb,),
        in_specs=[row, vec(), vec(), vec(), vec(), vec(), full(g1), full(g2), full(w_gate),
                  pair_spec, pair_spec, full(w_na), full(w_gq), full(w_out), full(w_router_t)],
        out_specs=[row, row, pl.BlockSpec((1, n_exp, tm), lambda i: (i // tpb, 0, i % tpb))],
        out_shape=[jax.ShapeDtypeStruct(x2.shape, F32), jax.ShapeDtypeStruct(x2.shape, F32),
                   jax.ShapeDtypeStruct((batch, n_exp, seq), F32)],
        compiler_params=_params("arbitrary"),
        name="merge",
    )(x2, *mods, g1, g2, w_gate, yna, ygq, w_na, w_gq, w_out, w_router_t)


def _topk_kernel(aff_ref, idx_ref, gate_ref, *, cap):
    n_exp, nblk, _ = aff_ref.shape[1:]
    n_bits = 31
    hi = lax.Precision.HIGHEST

    def search(it, ths):
        bit = jnp.left_shift(jnp.int32(1), n_bits - 1 - it)
        out = []
        for e in range(n_exp):
            bits = pltpu.bitcast(aff_ref[0, e], I32)
            cand = ths[e] | bit
            cnt = jnp.sum((bits >= cand).astype(F32), keepdims=True)
            out.append(jnp.where(cnt >= cap, cand, ths[e]))
        return tuple(out)

    ths = lax.fori_loop(0, n_bits, search, tuple(jnp.zeros((1, 1), I32) for _ in range(n_exp)))

    r_i = lax.broadcasted_iota(I32, (LANES, LANES), 0)
    c_i = lax.broadcasted_iota(I32, (LANES, LANES), 1)
    incl = (r_i <= c_i).astype(BF16)
    br = lax.broadcasted_iota(I32, (nblk, nblk), 0)
    bc = lax.broadcasted_iota(I32, (nblk, nblk), 1)
    strict_lower = (bc < br).astype(BF16)
    incl_blk = (br <= bc).astype(BF16)
    ones_rows = jnp.ones((SUBLANES, LANES), BF16)
    lane_vals = lax.broadcasted_iota(I32, (SUBLANES, LANES), 1).astype(BF16)
    blk_vals = lax.broadcasted_iota(I32, (SUBLANES, nblk), 1).astype(BF16)
    j_col = lax.broadcasted_iota(I32, (cap, nblk), 0).astype(F32)
    j_col_l = lax.broadcasted_iota(I32, (cap, LANES), 0).astype(F32)

    for e in range(n_exp):
        aff = aff_ref[0, e]
        bits = pltpu.bitcast(aff, I32)
        th = ths[e]
        gt = bits > th
        eq = bits == th
        need = cap - jnp.sum(gt.astype(F32), keepdims=True)
        eq_b = eq.astype(BF16)
        eq_cs = _dot(eq_b, incl)
        eq_tot = jnp.broadcast_to(eq_cs[:, LANES - 1:], (nblk, LANES)).astype(BF16)
        eq_rank = _dot(strict_lower, eq_tot) + eq_cs
        sel = gt | (eq & (eq_rank <= need))
        sel_b = sel.astype(BF16)
        cs = jnp.where(sel, _dot(sel_b, incl), 0.0)
        tot_row = _dot_nt(ones_rows, sel_b)
        cum_inc = _dot(tot_row.astype(BF16), incl_blk)[0:1]
        cum_exc = cum_inc - tot_row[0:1]
        onehot = (cum_exc <= j_col) & (j_col < cum_inc)
        onehot_b = onehot.astype(BF16)
        base = jnp.sum(jnp.where(onehot, cum_exc, 0.0), axis=-1, keepdims=True)
        local = j_col_l - base + 1.0
        g_cs = _dot(onehot_b, cs.astype(BF16))
        match = g_cs == local
        g_aff = _dot(onehot.astype(F32), aff, hi)
        blk_row = _dot_nt(blk_vals, onehot_b)
        lane_row = _dot_nt(lane_vals, match.astype(BF16))
        gate_row = _dot_nt(jnp.ones((SUBLANES, LANES), F32), jnp.where(match, g_aff, 0.0), hi)
        idx_ref[0, e:e + 1, :] = (blk_row[0:1] * LANES + lane_row[0:1]).astype(I32)
        gate_ref[0, e:e + 1, :] = gate_row[0:1]


def _topk(aff4, cap):
    batch, n_exp, nblk, _ = aff4.shape
    out_spec = pl.BlockSpec((1, n_exp, cap), lambda b: (b, 0, 0))
    return pl.pallas_call(
        functools.partial(_topk_kernel, cap=cap),
        grid=(batch,),
        in_specs=[pl.BlockSpec((1, n_exp, nblk, LANES), lambda b: (b, 0, 0, 0))],
        out_specs=[out_spec, out_spec],
        out_shape=[jax.ShapeDtypeStruct((batch, n_exp, cap), I32),
                   jax.ShapeDtypeStruct((batch, n_exp, cap), F32)],
        compiler_params=_params("arbitrary"),
        name="topk",
    )(aff4)


def _ffn_kernel(idx_ref, idx_next_ref, h2_ref, wg_ref, wu_ref, wd_ref, y_ref, xbuf, sem,
                *, n_exp, seq, row_chunk, ff_chunk):
    s = pl.program_id(0)
    n_steps = pl.num_programs(0)
    cap = xbuf.shape[1]
    ff = wg_ref.shape[2]
    slot = s % 2

    def row_copy(tok, i, slot_):
        return pltpu.make_async_copy(h2_ref.at[pl.ds(tok, 1), :], xbuf.at[slot_, pl.ds(i, 1), :], sem.at[slot_])

    def gather(ref, step, slot_):
        base = (step // n_exp) * seq

        def body(i, carry):
            row_copy(base + ref[0, 0, i], i, slot_).start()
            return carry

        lax.fori_loop(0, cap, body, 0, unroll=8)

    def wait_all(slot_):
        pltpu.make_async_copy(h2_ref.at[pl.ds(0, cap), :], xbuf.at[slot_], sem.at[slot_]).wait()

    @pl.when(s == 0)
    def _():
        gather(idx_ref, s, slot)

    wait_all(slot)

    nxt = jnp.minimum(s + 1, n_steps - 1)
    nxt_base = (nxt // n_exp) * seq
    groups = [(r0, f0) for r0 in range(0, cap, row_chunk) for f0 in range(0, ff, ff_chunk)]
    per_group = cap // len(groups)
    y = None
    for gi, (r0, f0) in enumerate(groups):
        for i in range(gi * per_group, (gi + 1) * per_group):
            row_copy(nxt_base + idx_next_ref[0, 0, i], i, 1 - slot).start()
        x = xbuf[slot, r0:r0 + row_chunk, :].astype(BF16)
        a = _dot(x, wg_ref[0, :, f0:f0 + ff_chunk])
        u = _dot(x, wu_ref[0, :, f0:f0 + ff_chunk])
        act = ((a * jax.nn.sigmoid(a)) * u).astype(BF16)
        part = _dot(act, wd_ref[0, f0:f0 + ff_chunk, :])
        y = part if f0 == 0 else y + part
        if f0 + ff_chunk == ff:
            y_ref[0, r0:r0 + row_chunk, :] = y

    @pl.when(s == n_steps - 1)
    def _():
        wait_all(1 - slot)


def _ffn(idx3, h2, wg, wu, wd, batch, seq):
    n_exp, d, ff = wg.shape
    cap = idx3.shape[2]
    n_steps = batch * n_exp
    kern = functools.partial(_ffn_kernel, n_exp=n_exp, seq=seq, row_chunk=min(512, cap), ff_chunk=min(1024, ff))
    smem = lambda f: pl.BlockSpec((1, 1, cap), f, memory_space=pltpu.SMEM)
    return pl.pallas_call(
        kern,
        grid=(n_steps,),
        in_specs=[smem(lambda s: (s, 0, 0)),
                  smem(lambda s: (jnp.minimum(s + 1, n_steps - 1), 0, 0)),
                  pl.BlockSpec(memory_space=pl.ANY),
                  pl.BlockSpec((1, d, ff), lambda s: (s % n_exp, 0, 0)),
                  pl.BlockSpec((1, d, ff), lambda s: (s % n_exp, 0, 0)),
                  pl.BlockSpec((1, ff, d), lambda s: (s % n_exp, 0, 0))],
        out_specs=pl.BlockSpec((1, cap, d), lambda s: (s, 0, 0)),
        out_shape=jax.ShapeDtypeStruct((n_steps, cap, d), F32),
        scratch_shapes=[pltpu.VMEM((2, cap, d), F32), pltpu.SemaphoreType.DMA((2,))],
        compiler_params=_params("arbitrary"),
        name="ffn",
    )(idx3, idx3, h2, wg, wu, wd)


def _combine_kernel(idx_ref, gate_ref, y_ref, gate2_ref, x1_ref, out_ref, acc, sem, *, n_exp, tc):
    b = pl.program_id(0)
    e = pl.program_id(1)
    c = pl.program_id(2)
    n_c = pl.num_programs(2)

    @pl.when((e == 0) & (c == 0))
    def _():
        cp = pltpu.make_async_copy(x1_ref.at[b], acc, sem.at[0])
        cp.start()
        cp.wait()

    gate2 = gate2_ref[0]
    sub = lax.broadcasted_iota(I32, (SUBLANES, acc.shape[1]), 0)

    def group(gi, carry):
        r0 = pl.multiple_of(gi * SUBLANES, SUBLANES)
        rows = y_ref[0, pl.ds(r0, SUBLANES), :] * gate2
        for r in range(SUBLANES):
            j = c * tc + r0 + r
            tok = idx_ref[0, 0, j]
            g = lax.bitcast_convert_type(gate_ref[0, 0, j], F32)
            base = pl.multiple_of((tok >> 3) << 3, SUBLANES)
            upd = jnp.where(sub == (tok & 7), rows[r:r + 1, :] * g, 0.0)
            acc[pl.ds(base, SUBLANES), :] = acc[pl.ds(base, SUBLANES), :] + upd
        return carry

    lax.fori_loop(0, tc // SUBLANES, group, 0)

    @pl.when((e == n_exp - 1) & (c == n_c - 1))
    def _():
        cp = pltpu.make_async_copy(acc, out_ref.at[b], sem.at[1])
        cp.start()
        cp.wait()


def _combine(idx3, gbits3, y, gate2, x1, batch, seq, tc):
    n_steps, cap, d = y.shape
    n_exp = n_steps // batch
    smem = lambda: pl.BlockSpec((1, 1, cap), lambda b, e, c: (b * n_exp + e, 0, 0), memory_space=pltpu.SMEM)
    return pl.pallas_call(
        functools.partial(_combine_kernel, n_exp=n_exp, tc=tc),
        grid=(batch, n_exp, cap // tc),
        in_specs=[smem(), smem(),
                  pl.BlockSpec((1, tc, d), lambda b, e, c: (b * n_exp + e, c, 0)),
                  pl.BlockSpec((1, 1, d), lambda b, e, c: (b, 0, 0)),
                  pl.BlockSpec(memory_space=pl.ANY)],
        out_specs=pl.BlockSpec(memory_space=pl.ANY),
        out_shape=jax.ShapeDtypeStruct((batch, seq, d), F32),
        scratch_shapes=[pltpu.VMEM((seq, d), F32), pltpu.SemaphoreType.DMA((2,))],
        compiler_params=_params("arbitrary", "arbitrary", "arbitrary"),
        name="combine",
    )(idx3, gbits3, y, gate2, x1)


def _final_kernel(x_ref, g_ref, o_ref):
    o_ref[...] = _rms(x_ref[...]) * g_ref[...]


def _final(x2, g, tm):
    n, d = x2.shape
    return pl.pallas_call(
        _final_kernel,
        grid=(n // tm,),
        in_specs=[pl.BlockSpec((tm, d), lambda i: (i, 0)), pl.BlockSpec((1, d), lambda i: (0, 0))],
        out_specs=pl.BlockSpec((tm, d), lambda i: (i, 0)),
        out_shape=jax.ShapeDtypeStruct((n, d), F32),
        compiler_params=_params("arbitrary"),
        name="final",
    )(x2, g)


def _rope_tables(seq):
    t = jnp.arange(seq)
    row = (t // GRID_W).astype(F32)
    col = (t % GRID_W).astype(F32)
    half = HEAD_DIM // 2
    inv_freq = ROPE_BASE ** (-jnp.arange(0, half, 2, dtype=F32) / half)
    ang = jnp.concatenate([row[:, None] * inv_freq[None], col[:, None] * inv_freq[None]], axis=-1)
    cos = jnp.repeat(jnp.cos(ang), 2, axis=-1)
    sin = jnp.repeat(jnp.sin(ang), 2, axis=-1) * jnp.tile(jnp.array([-1.0, 1.0], F32), half)
    return jnp.tile(cos, (1, PAIR)), jnp.tile(sin, (1, PAIR))


def _gqa_head_order():
    group = GQA_HEADS // GQA_KV_HEADS
    order = []
    for j in range(group):
        order += [j, group + j]
    return np.concatenate([np.arange(h * HEAD_DIM, (h + 1) * HEAD_DIM) for h in order])


def kernel(x, c, w_ada, b_ada, norm1_g, w_in, q_norm_g, k_norm_g, na_rpb, w_branch_na, w_branch_gqa,
           w_out, norm2_g, w_router, w_exp_gate, w_exp_up, w_exp_down, final_g):
    batch, seq, d = x.shape
    depth = w_ada.shape[0]
    n_exp = w_router.shape[2]
    cap = EC_CAPACITY_FACTOR * seq // n_exp
    na_w = NA_HEADS * HEAD_DIM
    gq_w = GQA_HEADS * HEAD_DIM
    kv_w = GQA_KV_HEADS * HEAD_DIM
    assert kv_w == LANES and GQA_HEADS // GQA_KV_HEADS == N_PAIRS and seq % LANES == 0

    cos_t, sin_t = _rope_tables(seq)
    perm = _gqa_head_order()
    x2 = x.reshape(batch * seq, d)
    c_pad = jnp.zeros((SUBLANES, d), F32).at[:batch].set(c)

    for l in range(depth):
        mod = _mod(c_pad, w_ada[l], b_ada[l][None, :])[:batch]
        shift1, scale1, gate1, shift2, scale2, gate2 = [m[:, None, :] for m in jnp.split(mod, 6, axis=-1)]

        w = w_in[l]
        q0 = 3 * na_w
        w_qkv = jnp.concatenate([w[:, :q0], w[:, q0:q0 + gq_w][:, perm], w[:, q0 + gq_w:q0 + gq_w + kv_w]],
                                axis=1).astype(BF16)
        w_vt = w[:, q0 + gq_w + kv_w:q0 + gq_w + 2 * kv_w].T.astype(BF16)
        w_gate = w[:, q0 + gq_w + 2 * kv_w:].astype(BF16)
        qg = jnp.tile(q_norm_g[l], PAIR)[None, :]
        kg = jnp.tile(k_norm_g[l], PAIR)[None, :]
        g1 = norm1_g[l][None, :]
        g2 = norm2_g[l][None, :]

        qna, kna, vna, qgq, kgq, vgqt = _proj(x2, shift1, scale1, g1, w_qkv, w_vt, cos_t, sin_t, qg, kg,
                                              batch, seq, tm=512)
        yna = _na(qna, kna, vna, _na_bias_table(na_rpb[l]))
        ygq = _gqa(qgq, kgq, vgqt, tq=128, kc=min(1024, seq))

        x1, h2, aff = _merge(x2, (shift1, scale1, gate1, shift2, scale2), g1, g2, w_gate, yna, ygq,
                             w_branch_na[l].astype(BF16), w_branch_gqa[l][perm].astype(BF16),
                             w_out[l].astype(BF16), w_router[l].T, batch, seq, tm=256)

        idx, gates = _topk(aff.reshape(batch, n_exp, seq // LANES, LANES), cap)
        idx3 = idx.reshape(batch * n_exp, 1, cap)
        gbits3 = lax.bitcast_convert_type(gates, I32).reshape(batch * n_exp, 1, cap)
        y = _ffn(idx3, h2, w_exp_gate[l].astype(BF16), w_exp_up[l].astype(BF16), w_exp_down[l].astype(BF16),
                 batch, seq)
        x2 = _combine(idx3, gbits3, y, gate2, x1.reshape(batch, seq, d), batch, seq,
                      tc=min(256, cap)).reshape(batch * seq, d)

    return _final(x2, final_g[None, :], tm=1024).reshape(batch, seq, d)
```

```python
import functools

import numpy as np
import jax
import jax.numpy as jnp
from jax import lax
from jax.experimental import pallas as pl
from jax.experimental.pallas import tpu as pltpu

F32 = jnp.float32
BF16 = jnp.bfloat16
I32 = jnp.int32

GRID_W = 64
HEAD_DIM = 64
NA_HEADS = 8
NA_WIN_ROWS = 8
NA_WIN_COLS = 16
GQA_HEADS = 8
GQA_KV_HEADS = 2
ROPE_BASE = 10000.0
N_EXPERTS = 16
EC_CAPACITY_FACTOR = 2
EPS = 1e-6

LANES = 128
SUBLANES = 8
PAIR = LANES // HEAD_DIM
N_PAIRS = NA_HEADS // PAIR
NEG = -1e30
LOG2E = 1.4426950408889634
VMEM_LIMIT = 56 * 1024 * 1024

_NT = (((1,), (1,)), ((), ()))


def _dot(a, b, precision=None):
    return jnp.dot(a, b, preferred_element_type=F32, precision=precision)


def _dot_nt(a, b, precision=None):
    return lax.dot_general(a, b, _NT, preferred_element_type=F32, precision=precision)


def _rms(x):
    return x * lax.rsqrt(jnp.mean(x * x, axis=-1, keepdims=True) + EPS)


def _params(*sem):
    return pltpu.CompilerParams(dimension_semantics=sem, vmem_limit_bytes=VMEM_LIMIT)


def _mod_kernel(c_ref, w_ref, b_ref, o_ref):
    c = c_ref[...]
    sc = c * jax.nn.sigmoid(c)
    o_ref[...] = _dot(sc, w_ref[...], lax.Precision.HIGHEST) + b_ref[...]


def _mod(c_pad, w_ada, b_ada):
    rows, d = c_pad.shape
    n = w_ada.shape[1]
    tn = 1024
    return pl.pallas_call(
        _mod_kernel,
        grid=(n // tn,),
        in_specs=[pl.BlockSpec((rows, d), lambda j: (0, 0)),
                  pl.BlockSpec((d, tn), lambda j: (0, j)),
                  pl.BlockSpec((1, tn), lambda j: (0, j))],
        out_specs=pl.BlockSpec((rows, tn), lambda j: (0, j)),
        out_shape=jax.ShapeDtypeStruct((rows, n), F32),
        compiler_params=_params("arbitrary"),
        name="mod",
    )(c_pad, w_ada, b_ada)


def _head_rms_rope(x, gain, cos, sin_signed):
    lane = lax.broadcasted_iota(I32, x.shape, 1)
    low = lane < HEAD_DIM
    ss = x * x
    s_lo = jnp.sum(jnp.where(low, ss, 0.0), axis=-1, keepdims=True)
    s_hi = jnp.sum(jnp.where(low, 0.0, ss), axis=-1, keepdims=True)
    r = jnp.where(low, lax.rsqrt(s_lo / HEAD_DIM + EPS), lax.rsqrt(s_hi / HEAD_DIM + EPS))
    y = (x * r) * gain
    nxt = pltpu.roll(y, LANES - 1, 1)
    prv = pltpu.roll(y, 1, 1)
    partner = jnp.where(lane % 2 == 0, nxt, prv)
    return y * cos + partner * sin_signed


def _proj_kernel(x_ref, shift_ref, scale_ref, g_ref, w_ref, wvt_ref, cos_ref, sin_ref, qg_ref, kg_ref,
                 qna_ref, kna_ref, vna_ref, qgq_ref, kgq_ref, vgqt_ref):
    x = x_ref[...]
    h = ((_rms(x) * g_ref[...]) * (1.0 + scale_ref[0]) + shift_ref[0]).astype(BF16)
    proj = _dot(h, w_ref[...])
    vgqt_ref[0] = _dot_nt(wvt_ref[...], h).astype(BF16)
    scale = HEAD_DIM ** -0.5
    na_w = N_PAIRS * LANES
    for j in range(N_PAIRS):
        c0 = j * LANES
        qna_ref[0, j] = (proj[:, c0:c0 + LANES] * scale).astype(BF16)
        kna_ref[0, j] = proj[:, na_w + c0:na_w + c0 + LANES].astype(BF16)
        vna_ref[0, j] = proj[:, 2 * na_w + c0:2 * na_w + c0 + LANES].astype(BF16)
    cos = cos_ref[...]
    sin = sin_ref[...]
    base = 3 * na_w
    for j in range(N_PAIRS):
        c0 = base + j * LANES
        q = _head_rms_rope(proj[:, c0:c0 + LANES], qg_ref[...], cos, sin)
        qgq_ref[0, j] = (q * (scale * LOG2E)).astype(BF16)
    c0 = base + N_PAIRS * LANES
    kgq_ref[0] = _head_rms_rope(proj[:, c0:c0 + LANES], kg_ref[...], cos, sin).astype(BF16)


def _proj(x2, shift1, scale1, g1, w_qkv, w_vt, cos_t, sin_t, qg, kg, batch, seq, tm):
    d = x2.shape[1]
    tpb = seq // tm
    pair_shape = jax.ShapeDtypeStruct((batch, N_PAIRS, seq, LANES), BF16)
    pair_spec = pl.BlockSpec((1, N_PAIRS, tm, LANES), lambda i: (i // tpb, 0, i % tpb, 0))
    vec = lambda: pl.BlockSpec((1, 1, d), lambda i: (i // tpb, 0, 0))
    return pl.pallas_call(
        _proj_kernel,
        grid=(batch * tpb,),
        in_specs=[pl.BlockSpec((tm, d), lambda i: (i, 0)),
                  vec(), vec(),
                  pl.BlockSpec((1, d), lambda i: (0, 0)),
                  pl.BlockSpec(w_qkv.shape, lambda i: (0, 0)),
                  pl.BlockSpec(w_vt.shape, lambda i: (0, 0)),
                  pl.BlockSpec((tm, LANES), lambda i: (i % tpb, 0)),
                  pl.BlockSpec((tm, LANES), lambda i: (i % tpb, 0)),
                  pl.BlockSpec((1, LANES), lambda i: (0, 0)),
                  pl.BlockSpec((1, LANES), lambda i: (0, 0))],
        out_specs=[pair_spec, pair_spec, pair_spec, pair_spec,
                   pl.BlockSpec((1, tm, LANES), lambda i: (i // tpb, i % tpb, 0)),
                   pl.BlockSpec((1, LANES, tm), lambda i: (i // tpb, 0, i % tpb))],
        out_shape=[pair_shape, pair_shape, pair_shape, pair_shape,
                   jax.ShapeDtypeStruct((batch, seq, LANES), BF16),
                   jax.ShapeDtypeStruct((batch, LANES, seq), BF16)],
        compiler_params=_params("arbitrary"),
        name="proj",
    )(x2, shift1, scale1, g1, w_qkv, w_vt, cos_t, sin_t, qg, kg)


def _na_kernel(q_ref, k_ref, v_ref, bias_ref, o_ref, *, rows):
    band = NA_WIN_ROWS * GRID_W
    rows_per_trip = 8
    assert rows % rows_per_trip == 0
    lane = lax.broadcasted_iota(I32, (GRID_W, LANES), 1)
    low = lane < HEAD_DIM

    def body(rb, carry):
        staged = []
        for t in range(rows_per_trip):
            r = rb * rows_per_trip + t
            rs = jnp.clip(r - NA_WIN_ROWS // 2, 0, rows - NA_WIN_ROWS)
            lo = rs - r + (NA_WIN_ROWS - 1)
            q = q_ref[0, 0, pl.ds(pl.multiple_of(r * GRID_W, GRID_W), GRID_W), :]
            kb = k_ref[0, 0, pl.ds(pl.multiple_of(rs * GRID_W, GRID_W), band), :]
            zero = jnp.zeros_like(q)
            q2 = jnp.concatenate([jnp.where(low, q, zero), jnp.where(low, zero, q)], axis=0)
            staged.append((r, rs, _dot_nt(q2, kb) + bias_ref[0, lo]))
        for r, rs, s in staged:
            vb = v_ref[0, 0, pl.ds(pl.multiple_of(rs * GRID_W, GRID_W), band), :]
            m = jnp.max(s, axis=-1, keepdims=True)
            p = jnp.exp(s - m)
            l = jnp.sum(p, axis=-1, keepdims=True)
            o2 = _dot(p.astype(BF16), vb) / l
            o = jnp.where(low, o2[:GRID_W], o2[GRID_W:])
            o_ref[0, 0, pl.ds(pl.multiple_of(r * GRID_W, GRID_W), GRID_W), :] = o.astype(o_ref.dtype)
        return carry

    lax.fori_loop(0, rows // rows_per_trip, body, 0)


def _na(q, k, v, bias):
    batch, n_pairs, seq, _ = q.shape
    rows = seq // GRID_W
    spec = pl.BlockSpec((1, 1, seq, LANES), lambda b, j: (b, j, 0, 0))
    return pl.pallas_call(
        functools.partial(_na_kernel, rows=rows),
        grid=(batch, n_pairs),
        in_specs=[spec, spec, spec,
                  pl.BlockSpec((1,) + bias.shape[1:], lambda b, j: (j, 0, 0, 0))],
        out_specs=spec,
        out_shape=jax.ShapeDtypeStruct(q.shape, BF16),
        compiler_params=_params("arbitrary", "arbitrary"),
        name="na",
    )(q, k, v, bias)


def _na_bias_table(rpb):
    cols = np.arange(GRID_W)
    col_start = np.clip(cols - NA_WIN_COLS // 2, 0, GRID_W - NA_WIN_COLS)
    kc = np.arange(GRID_W)
    valid = (kc[None, :] >= col_start[:, None]) & (kc[None, :] < col_start[:, None] + NA_WIN_COLS)
    dc = np.clip(kc[None, :] - cols[:, None] + (NA_WIN_COLS - 1), 0, 2 * NA_WIN_COLS - 2)
    lo = np.arange(NA_WIN_ROWS)
    jb = np.arange(NA_WIN_ROWS)
    dr = lo[:, None] + jb[None, :]
    t = rpb[:, dr]
    t = t[:, :, :, dc]
    t = jnp.where(jnp.asarray(valid)[None, None, None], t, NEG)
    t = t.transpose(0, 1, 3, 2, 4).reshape(NA_HEADS, NA_WIN_ROWS, GRID_W, NA_WIN_ROWS * GRID_W)
    t = t.reshape(N_PAIRS, PAIR, NA_WIN_ROWS, GRID_W, NA_WIN_ROWS * GRID_W).transpose(0, 2, 1, 3, 4)
    return t.reshape(N_PAIRS, NA_WIN_ROWS, PAIR * GRID_W, NA_WIN_ROWS * GRID_W).astype(F32)


def _col_reduce(op, x, ways=64):
    rows, n = x.shape
    part = op(x.reshape(rows // ways, ways, n), axis=0)
    return op(part, axis=0, keepdims=True)


def _gqa_kernel(q_ref, k_ref, vt_ref, o_ref, *, kc):
    seq = k_ref.shape[1]
    tq = q_ref.shape[2]
    lane = lax.broadcasted_iota(I32, (tq, LANES), 1)
    low = lane < HEAD_DIM

    def masked_q(j):
        q = q_ref[0, j]
        zero = jnp.zeros_like(q)
        return jnp.concatenate([jnp.where(low, q, zero), jnp.where(low, zero, q)], axis=0)

    def scores(step):
        j, c0 = step
        return _dot_nt(k_ref[0, c0:c0 + kc, :], masked_q(j))

    steps = [(j, c0) for c0 in range(0, seq, kc) for j in range(N_PAIRS)]
    ahead = 2
    pending = [scores(s) for s in steps[:ahead]]
    state = {}
    for i, (j, c0) in enumerate(steps):
        st = pending.pop(0)
        if i + ahead < len(steps):
            pending.append(scores(steps[i + ahead]))
        if c0 == 0:
            state[j] = (jnp.full((1, 2 * tq), NEG, F32), jnp.zeros((1, 2 * tq), F32),
                        jnp.zeros((LANES, 2 * tq), F32))
        m, l, acc = state[j]
        m_new = jnp.maximum(m, _col_reduce(jnp.max, st))
        alpha = jnp.exp2(m - m_new)
        p = jnp.exp2(st - m_new)
        l = alpha * l + _col_reduce(jnp.sum, p)
        acc = alpha * acc + _dot(vt_ref[0, :, c0:c0 + kc], p.astype(BF16))
        state[j] = (m_new, l, acc)
        if c0 + kc == seq:
            ot = acc / l
            o_t = jnp.concatenate([ot[:HEAD_DIM, :tq], ot[HEAD_DIM:, tq:]], axis=0)
            o_ref[0, j] = o_t.T.astype(o_ref.dtype)


def _gqa(q, k, vt, tq, kc):
    batch, n_pairs, seq, _ = q.shape
    qspec = pl.BlockSpec((1, n_pairs, tq, LANES), lambda b, i: (b, 0, i, 0))
    return pl.pallas_call(
        functools.partial(_gqa_kernel, kc=kc),
        grid=(batch, seq // tq),
        in_specs=[qspec,
                  pl.BlockSpec((1, seq, LANES), lambda b, i: (b, 0, 0)),
                  pl.BlockSpec((1, LANES, seq), lambda b, i: (b, 0, 0))],
        out_specs=qspec,
        out_shape=jax.ShapeDtypeStruct(q.shape, BF16),
        compiler_params=_params("arbitrary", "arbitrary"),
        name="gqa",
    )(q, k, vt)


def _merge_kernel(x_ref, shift1_ref, scale1_ref, gate1_ref, shift2_ref, scale2_ref, g1_ref, g2_ref,
                  wgate_ref, yna_ref, ygq_ref, wna_ref, wgq_ref, wout_ref, wr_ref,
                  x1_ref, h2_ref, aff_ref):
    tm, d = x_ref.shape
    n_parts = 2
    rp = tm // n_parts
    parts = [slice(i * rp, (i + 1) * rp) for i in range(n_parts)]
    xs = [x_ref[r, :] for r in parts]
    hs = [((_rms(x) * g1_ref[...]) * (1.0 + scale1_ref[0]) + shift1_ref[0]).astype(BF16) for x in xs]
    logit_g = [_dot(h, wgate_ref[...]) for h in hs]
    ynas = [jnp.concatenate([yna_ref[0, j, r, :] for j in range(N_PAIRS)], axis=-1) for r in parts]
    ygqs = [jnp.concatenate([ygq_ref[0, j, r, :] for j in range(N_PAIRS)], axis=-1) for r in parts]
    br_na = [_dot(y, wna_ref[...]) for y in ynas]
    br_gq = [_dot(y, wgq_ref[...]) for y in ygqs]
    merged = []
    for lg, a, b in zip(logit_g, br_na, br_gq):
        gates = jax.nn.sigmoid(lg)
        merged.append((gates[:, :d] * a + gates[:, d:] * b).astype(BF16))
    outs = [_dot(mg, wout_ref[...]) for mg in merged]
    h2s = []
    for r, x, o in zip(parts, xs, outs):
        x1 = x + gate1_ref[0] * o
        x1_ref[r, :] = x1
        h2 = (_rms(x1) * g2_ref[...]) * (1.0 + scale2_ref[0]) + shift2_ref[0]
        h2_ref[r, :] = h2
        h2s.append(h2)
    for r, h2 in zip(parts, h2s):
        logits = _dot_nt(wr_ref[...], h2, lax.Precision.HIGHEST)
        z = jnp.exp(logits - jnp.max(logits, axis=0, keepdims=True))
        aff_ref[0, :, r] = z / jnp.sum(z, axis=0, keepdims=True)


def _merge(x2, mods, g1, g2, w_gate, yna, ygq, w_na, w_gq, w_out, w_router_t, batch, seq, tm):
    d = x2.shape[1]
    tpb = seq // tm
    n_exp = w_router_t.shape[0]
    vec = lambda: pl.BlockSpec((1, 1, d), lambda i: (i // tpb, 0, 0))
    full = lambda a: pl.BlockSpec(a.shape, lambda i: (0,) * a.ndim)
    row = pl.BlockSpec((tm, d), lambda i: (i, 0))
    pair_spec = pl.BlockSpec((1, N_PAIRS, tm, LANES), lambda i: (i // tpb, 0, i % tpb, 0))
    return pl.pallas_call(
        _merge_kernel,
        grid=(batch * tpb,),
        in_specs=[row, vec(), vec(), vec(), vec(), vec(), full(g1), full(g2), full(w_gate),
                  pair_spec, pair_spec, full(w_na), full(w_gq), full(w_out), full(w_router_t)],
        out_specs=[row, row, pl.BlockSpec((1, n_exp, tm), lambda i: (i // tpb, 0, i % tpb))],
        out_shape=[jax.ShapeDtypeStruct(x2.shape, F32), jax.ShapeDtypeStruct(x2.shape, F32),
                   jax.ShapeDtypeStruct((batch, n_exp, seq), F32)],
        compiler_params=_params("arbitrary"),
        name="merge",
    )(x2, *mods, g1, g2, w_gate, yna, ygq, w_na, w_gq, w_out, w_router_t)


def _topk_kernel(aff_ref, idx_ref, gate_ref, *, cap):
    n_exp, nblk, _ = aff_ref.shape[1:]
    n_bits = 31
    hi = lax.Precision.HIGHEST

    def search(it, ths):
        bit = jnp.left_shift(jnp.int32(1), n_bits - 1 - it)
        out = []
        for e in range(n_exp):
            bits = pltpu.bitcast(aff_ref[0, e], I32)
            cand = ths[e] | bit
            cnt = jnp.sum((bits >= cand).astype(F32), keepdims=True)
            out.append(jnp.where(cnt >= cap, cand, ths[e]))
        return tuple(out)

    ths = lax.fori_loop(0, n_bits, search, tuple(jnp.zeros((1, 1), I32) for _ in range(n_exp)))

    r_i = lax.broadcasted_iota(I32, (LANES, LANES), 0)
    c_i = lax.broadcasted_iota(I32, (LANES, LANES), 1)
    incl = (r_i <= c_i).astype(BF16)
    br = lax.broadcasted_iota(I32, (nblk, nblk), 0)
    bc = lax.broadcasted_iota(I32, (nblk, nblk), 1)
    strict_lower = (bc < br).astype(BF16)
    incl_blk = (br <= bc).astype(BF16)
    ones_rows = jnp.ones((SUBLANES, LANES), BF16)
    lane_vals = lax.broadcasted_iota(I32, (SUBLANES, LANES), 1).astype(BF16)
    blk_vals = lax.broadcasted_iota(I32, (SUBLANES, nblk), 1).astype(BF16)
    j_col = lax.broadcasted_iota(I32, (cap, nblk), 0).astype(F32)
    j_col_l = lax.broadcasted_iota(I32, (cap, LANES), 0).astype(F32)

    for e in range(n_exp):
        aff = aff_ref[0, e]
        bits = pltpu.bitcast(aff, I32)
        th = ths[e]
        gt = bits > th
        eq = bits == th
        need = cap - jnp.sum(gt.astype(F32), keepdims=True)
        eq_b = eq.astype(BF16)
        eq_cs = _dot(eq_b, incl)
        eq_tot = jnp.broadcast_to(eq_cs[:, LANES - 1:], (nblk, LANES)).astype(BF16)
        eq_rank = _dot(strict_lower, eq_tot) + eq_cs
        sel = gt | (eq & (eq_rank <= need))
        sel_b = sel.astype(BF16)
        cs = jnp.where(sel, _dot(sel_b, incl), 0.0)
        tot_row = _dot_nt(ones_rows, sel_b)
        cum_inc = _dot(tot_row.astype(BF16), incl_blk)[0:1]
        cum_exc = cum_inc - tot_row[0:1]
        onehot = (cum_exc <= j_col) & (j_col < cum_inc)
        onehot_b = onehot.astype(BF16)
        base = jnp.sum(jnp.where(onehot, cum_exc, 0.0), axis=-1, keepdims=True)
        local = j_col_l - base + 1.0
        g_cs = _dot(onehot_b, cs.astype(BF16))
        match = g_cs == local
        g_aff = _dot(onehot.astype(F32), aff, hi)
        blk_row = _dot_nt(blk_vals, onehot_b)
        lane_row = _dot_nt(lane_vals, match.astype(BF16))
        gate_row = _dot_nt(jnp.ones((SUBLANES, LANES), F32), jnp.where(match, g_aff, 0.0), hi)
        idx_ref[0, e:e + 1, :] = (blk_row[0:1] * LANES + lane_row[0:1]).astype(I32)
        gate_ref[0, e:e + 1, :] = gate_row[0:1]


def _topk(aff4, cap):
    batch, n_exp, nblk, _ = aff4.shape
    out_spec = pl.BlockSpec((1, n_exp, cap), lambda b: (b, 0, 0))
    return pl.pallas_call(
        functools.partial(_topk_kernel, cap=cap),
        grid=(batch,),
        in_specs=[pl.BlockSpec((1, n_exp, nblk, LANES), lambda b: (b, 0, 0, 0))],
        out_specs=[out_spec, out_spec],
        out_shape=[jax.ShapeDtypeStruct((batch, n_exp, cap), I32),
                   jax.ShapeDtypeStruct((batch, n_exp, cap), F32)],
        compiler_params=_params("arbitrary"),
        name="topk",
    )(aff4)


def _ffn_kernel(idx_ref, idx_next_ref, h2_ref, wg_ref, wu_ref, wd_ref, y_ref, xbuf, sem,
                *, n_exp, seq, row_chunk, ff_chunk):
    s = pl.program_id(0)
    n_steps = pl.num_programs(0)
    cap = xbuf.shape[1]
    ff = wg_ref.shape[2]
    slot = s % 2

    def row_copy(tok, i, slot_):
        return pltpu.make_async_copy(h2_ref.at[pl.ds(tok, 1), :], xbuf.at[slot_, pl.ds(i, 1), :], sem.at[slot_])

    def gather(ref, step, slot_):
        base = (step // n_exp) * seq

        def body(i, carry):
            row_copy(base + ref[0, 0, i], i, slot_).start()
            return carry

        lax.fori_loop(0, cap, body, 0, unroll=8)

    def wait_all(slot_):
        pltpu.make_async_copy(h2_ref.at[pl.ds(0, cap), :], xbuf.at[slot_], sem.at[slot_]).wait()

    @pl.when(s == 0)
    def _():
        gather(idx_ref, s, slot)

    wait_all(slot)

    nxt = jnp.minimum(s + 1, n_steps - 1)
    nxt_base = (nxt // n_exp) * seq
    groups = [(r0, f0) for r0 in range(0, cap, row_chunk) for f0 in range(0, ff, ff_chunk)]
    per_group = cap // len(groups)
    y = None
    for gi, (r0, f0) in enumerate(groups):
        for i in range(gi * per_group, (gi + 1) * per_group):
            row_copy(nxt_base + idx_next_ref[0, 0, i], i, 1 - slot).start()
        x = xbuf[slot, r0:r0 + row_chunk, :].astype(BF16)
        a = _dot(x, wg_ref[0, :, f0:f0 + ff_chunk])
        u = _dot(x, wu_ref[0, :, f0:f0 + ff_chunk])
        act = ((a * jax.nn.sigmoid(a)) * u).astype(BF16)
        part = _dot(act, wd_ref[0, f0:f0 + ff_chunk, :])
        y = part if f0 == 0 else y + part
        if f0 + ff_chunk == ff:
            y_ref[0, r0:r0 + row_chunk, :] = y

    @pl.when(s == n_steps - 1)
    def _():
        wait_all(1 - slot)


def _ffn(idx3, h2, wg, wu, wd, batch, seq):
    n_exp, d, ff = wg.shape
    cap = idx3.shape[2]
    n_steps = batch * n_exp
    kern = functools.partial(_ffn_kernel, n_exp=n_exp, seq=seq, row_chunk=min(512, cap), ff_chunk=min(1024, ff))
    smem = lambda f: pl.BlockSpec((1, 1, cap), f, memory_space=pltpu.SMEM)
    return pl.pallas_call(
        kern,
        grid=(n_steps,),
        in_specs=[smem(lambda s: (s, 0, 0)),
                  smem(lambda s: (jnp.minimum(s + 1, n_steps - 1), 0, 0)),
                  pl.BlockSpec(memory_space=pl.ANY),
                  pl.BlockSpec((1, d, ff), lambda s: (s % n_exp, 0, 0)),
                  pl.BlockSpec((1, d, ff), lambda s: (s % n_exp, 0, 0)),
                  pl.BlockSpec((1, ff, d), lambda s: (s % n_exp, 0, 0))],
        out_specs=pl.BlockSpec((1, cap, d), lambda s: (s, 0, 0)),
        out_shape=jax.ShapeDtypeStruct((n_steps, cap, d), F32),
        scratch_shapes=[pltpu.VMEM((2, cap, d), F32), pltpu.SemaphoreType.DMA((2,))],
        compiler_params=_params("arbitrary"),
        name="ffn",
    )(idx3, idx3, h2, wg, wu, wd)


def _combine_kernel(idx_ref, gate_ref, y_ref, gate2_ref, x1_ref, out_ref, acc, sem, *, n_exp, tc):
    b = pl.program_id(0)
    e = pl.program_id(1)
    c = pl.program_id(2)
    n_c = pl.num_programs(2)

    @pl.when((e == 0) & (c == 0))
    def _():
        cp = pltpu.make_async_copy(x1_ref.at[b], acc, sem.at[0])
        cp.start()
        cp.wait()

    gate2 = gate2_ref[0]
    sub = lax.broadcasted_iota(I32, (SUBLANES, acc.shape[1]), 0)

    def group(gi, carry):
        r0 = pl.multiple_of(gi * SUBLANES, SUBLANES)
        rows = y_ref[0, pl.ds(r0, SUBLANES), :] * gate2
        for r in range(SUBLANES):
            j = c * tc + r0 + r
            tok = idx_ref[0, 0, j]
            g = lax.bitcast_convert_type(gate_ref[0, 0, j], F32)
            base = pl.multiple_of((tok >> 3) << 3, SUBLANES)
            upd = jnp.where(sub == (tok & 7), rows[r:r + 1, :] * g, 0.0)
            acc[pl.ds(base, SUBLANES), :] = acc[pl.ds(base, SUBLANES), :] + upd
        return carry

    lax.fori_loop(0, tc // SUBLANES, group, 0)

    @pl.when((e == n_exp - 1) & (c == n_c - 1))
    def _():
        cp = pltpu.make_async_copy(acc, out_ref.at[b], sem.at[1])
        cp.start()
        cp.wait()


def _combine(idx3, gbits3, y, gate2, x1, batch, seq, tc):
    n_steps, cap, d = y.shape
    n_exp = n_steps // batch
    smem = lambda: pl.BlockSpec((1, 1, cap), lambda b, e, c: (b * n_exp + e, 0, 0), memory_space=pltpu.SMEM)
    return pl.pallas_call(
        functools.partial(_combine_kernel, n_exp=n_exp, tc=tc),
        grid=(batch, n_exp, cap // tc),
        in_specs=[smem(), smem(),
                  pl.BlockSpec((1, tc, d), lambda b, e, c: (b * n_exp + e, c, 0)),
                  pl.BlockSpec((1, 1, d), lambda b, e, c: (b, 0, 0)),
                  pl.BlockSpec(memory_space=pl.ANY)],
        out_specs=pl.BlockSpec(memory_space=pl.ANY),
        out_shape=jax.ShapeDtypeStruct((batch, seq, d), F32),
        scratch_shapes=[pltpu.VMEM((seq, d), F32), pltpu.SemaphoreType.DMA((2,))],
        compiler_params=_params("arbitrary", "arbitrary", "arbitrary"),
        name="combine",
    )(idx3, gbits3, y, gate2, x1)


def _final_kernel(x_ref, g_ref, o_ref):
    o_ref[...] = _rms(x_ref[...]) * g_ref[...]


def _final(x2, g, tm):
    n, d = x2.shape
    return pl.pallas_call(
        _final_kernel,
        grid=(n // tm,),
        in_specs=[pl.BlockSpec((tm, d), lambda i: (i, 0)), pl.BlockSpec((1, d), lambda i: (0, 0))],
        out_specs=pl.BlockSpec((tm, d), lambda i: (i, 0)),
        out_shape=jax.ShapeDtypeStruct((n, d), F32),
        compiler_params=_params("arbitrary"),
        name="final",
    )(x2, g)


def _rope_tables(seq):
    t = jnp.arange(seq)
    row = (t // GRID_W).astype(F32)
    col = (t % GRID_W).astype(F32)
    half = HEAD_DIM // 2
    inv_freq = ROPE_BASE ** (-jnp.arange(0, half, 2, dtype=F32) / half)
    ang = jnp.concatenate([row[:, None] * inv_freq[None], col[:, None] * inv_freq[None]], axis=-1)
    cos = jnp.repeat(jnp.cos(ang), 2, axis=-1)
    sin = jnp.repeat(jnp.sin(ang), 2, axis=-1) * jnp.tile(jnp.array([-1.0, 1.0], F32), half)
    return jnp.tile(cos, (1, PAIR)), jnp.tile(sin, (1, PAIR))


def _gqa_head_order():
    group = GQA_HEADS // GQA_KV_HEADS
    order = []
    for j in range(group):
        order += [j, group + j]
    return np.concatenate([np.arange(h * HEAD_DIM, (h + 1) * HEAD_DIM) for h in order])


def kernel(x, c, w_ada, b_ada, norm1_g, w_in, q_norm_g, k_norm_g, na_rpb, w_branch_na, w_branch_gqa,
           w_out, norm2_g, w_router, w_exp_gate, w_exp_up, w_exp_down, final_g):
    batch, seq, d = x.shape
    depth = w_ada.shape[0]
    n_exp = w_router.shape[2]
    cap = EC_CAPACITY_FACTOR * seq // n_exp
    na_w = NA_HEADS * HEAD_DIM
    gq_w = GQA_HEADS * HEAD_DIM
    kv_w = GQA_KV_HEADS * HEAD_DIM
    assert kv_w == LANES and GQA_HEADS // GQA_KV_HEADS == N_PAIRS and seq % LANES == 0

    cos_t, sin_t = _rope_tables(seq)
    perm = _gqa_head_order()
    x2 = x.reshape(batch * seq, d)
    c_pad = jnp.zeros((SUBLANES, d), F32).at[:batch].set(c)

    for l in range(depth):
        mod = _mod(c_pad, w_ada[l], b_ada[l][None, :])[:batch]
        shift1, scale1, gate1, shift2, scale2, gate2 = [m[:, None, :] for m in jnp.split(mod, 6, axis=-1)]

        w = w_in[l]
        q0 = 3 * na_w
        w_qkv = jnp.concatenate([w[:, :q0], w[:, q0:q0 + gq_w][:, perm], w[:, q0 + gq_w:q0 + gq_w + kv_w]],
                                axis=1).astype(BF16)
        w_vt = w[:, q0 + gq_w + kv_w:q0 + gq_w + 2 * kv_w].T.astype(BF16)
        w_gate = w[:, q0 + gq_w + 2 * kv_w:].astype(BF16)
        qg = jnp.tile(q_norm_g[l], PAIR)[None, :]
        kg = jnp.tile(k_norm_g[l], PAIR)[None, :]
        g1 = norm1_g[l][None, :]
        g2 = norm2_g[l][None, :]

        qna, kna, vna, qgq, kgq, vgqt = _proj(x2, shift1, scale1, g1, w_qkv, w_vt, cos_t, sin_t, qg, kg,
                                              batch, seq, tm=512)
        yna = _na(qna, kna, vna, _na_bias_table(na_rpb[l]))
        ygq = _gqa(qgq, kgq, vgqt, tq=128, kc=min(1024, seq))

        x1, h2, aff = _merge(x2, (shift1, scale1, gate1, shift2, scale2), g1, g2, w_gate, yna, ygq,
                             w_branch_na[l].astype(BF16), w_branch_gqa[l][perm].astype(BF16),
                             w_out[l].astype(BF16), w_router[l].T, batch, seq, tm=512)

        idx, gates = _topk(aff.reshape(batch, n_exp, seq // LANES, LANES), cap)
        idx3 = idx.reshape(batch * n_exp, 1, cap)
        gbits3 = lax.bitcast_convert_type(gates, I32).reshape(batch * n_exp, 1, cap)
        y = _ffn(idx3, h2, w_exp_gate[l].astype(BF16), w_exp_up[l].astype(BF16), w_exp_down[l].astype(BF16),
                 batch, seq)
        x2 = _combine(idx3, gbits3, y, gate2, x1.reshape(batch, seq, d), batch, seq,
                      tc=min(256, cap)).reshape(batch * seq, d)

    return _final(x2, final_g[None, :], tm=1024).reshape(batch, seq, d)
```

```python
import functools

import numpy as np
import jax
import jax.numpy as jnp
from jax import lax
from jax.experimental import pallas as pl
from jax.experimental.pallas import tpu as pltpu

F32 = jnp.float32
BF16 = jnp.bfloat16
I32 = jnp.int32

GRID_W = 64
HEAD_DIM = 64
NA_HEADS = 8
NA_WIN_ROWS = 8
NA_WIN_COLS = 16
GQA_HEADS = 8
GQA_KV_HEADS = 2
ROPE_BASE = 10000.0
N_EXPERTS = 16
EC_CAPACITY_FACTOR = 2
EPS = 1e-6

LANES = 128
SUBLANES = 8
PAIR = LANES // HEAD_DIM
N_PAIRS = NA_HEADS // PAIR
NEG = -1e30
LOG2E = 1.4426950408889634
VMEM_LIMIT = 56 * 1024 * 1024

_NT = (((1,), (1,)), ((), ()))


def _dot(a, b, precision=None):
    return jnp.dot(a, b, preferred_element_type=F32, precision=precision)


def _dot_nt(a, b, precision=None):
    return lax.dot_general(a, b, _NT, preferred_element_type=F32, precision=precision)


def _rms(x):
    return x * lax.rsqrt(jnp.mean(x * x, axis=-1, keepdims=True) + EPS)


def _params(*sem):
    return pltpu.CompilerParams(dimension_semantics=sem, vmem_limit_bytes=VMEM_LIMIT)


def _mod_kernel(c_ref, w_ref, b_ref, o_ref):
    c = c_ref[...]
    sc = c * jax.nn.sigmoid(c)
    o_ref[...] = _dot(sc, w_ref[...], lax.Precision.HIGHEST) + b_ref[...]


def _mod(c_pad, w_ada, b_ada):
    rows, d = c_pad.shape
    n = w_ada.shape[1]
    tn = 1024
    return pl.pallas_call(
        _mod_kernel,
        grid=(n // tn,),
        in_specs=[pl.BlockSpec((rows, d), lambda j: (0, 0)),
                  pl.BlockSpec((d, tn), lambda j: (0, j)),
                  pl.BlockSpec((1, tn), lambda j: (0, j))],
        out_specs=pl.BlockSpec((rows, tn), lambda j: (0, j)),
        out_shape=jax.ShapeDtypeStruct((rows, n), F32),
        compiler_params=_params("arbitrary"),
        name="mod",
    )(c_pad, w_ada, b_ada)


def _head_rms_rope(x, gain, cos, sin_signed):
    lane = lax.broadcasted_iota(I32, x.shape, 1)
    low = lane < HEAD_DIM
    ss = x * x
    s_lo = jnp.sum(jnp.where(low, ss, 0.0), axis=-1, keepdims=True)
    s_hi = jnp.sum(jnp.where(low, 0.0, ss), axis=-1, keepdims=True)
    r = jnp.where(low, lax.rsqrt(s_lo / HEAD_DIM + EPS), lax.rsqrt(s_hi / HEAD_DIM + EPS))
    y = (x * r) * gain
    nxt = pltpu.roll(y, LANES - 1, 1)
    prv = pltpu.roll(y, 1, 1)
    partner = jnp.where(lane % 2 == 0, nxt, prv)
    return y * cos + partner * sin_signed


def _proj_kernel(x_ref, shift_ref, scale_ref, g_ref, w_ref, wvt_ref, cos_ref, sin_ref, qg_ref, kg_ref,
                 qna_ref, kna_ref, vna_ref, qgq_ref, kgq_ref, vgqt_ref):
    tm = x_ref.shape[0]
    n_parts = 2
    rp = tm // n_parts
    parts = [slice(i * rp, (i + 1) * rp) for i in range(n_parts)]
    hs = [((_rms(x_ref[r, :]) * g_ref[...]) * (1.0 + scale_ref[0]) + shift_ref[0]).astype(BF16) for r in parts]
    projs = [_dot(h, w_ref[...]) for h in hs]
    for r, h in zip(parts, hs):
        vgqt_ref[0, :, r] = _dot_nt(wvt_ref[...], h).astype(BF16)
    scale = HEAD_DIM ** -0.5
    na_w = N_PAIRS * LANES
    base = 3 * na_w
    for r, proj in zip(parts, projs):
        for j in range(N_PAIRS):
            c0 = j * LANES
            qna_ref[0, j, r, :] = (proj[:, c0:c0 + LANES] * scale).astype(BF16)
            kna_ref[0, j, r, :] = proj[:, na_w + c0:na_w + c0 + LANES].astype(BF16)
            vna_ref[0, j, r, :] = proj[:, 2 * na_w + c0:2 * na_w + c0 + LANES].astype(BF16)
        cos = cos_ref[r, :]
        sin = sin_ref[r, :]
        for j in range(N_PAIRS):
            c0 = base + j * LANES
            q = _head_rms_rope(proj[:, c0:c0 + LANES], qg_ref[...], cos, sin)
            qgq_ref[0, j, r, :] = (q * (scale * LOG2E)).astype(BF16)
        c0 = base + N_PAIRS * LANES
        kgq_ref[0, r, :] = _head_rms_rope(proj[:, c0:c0 + LANES], kg_ref[...], cos, sin).astype(BF16)


def _proj(x2, shift1, scale1, g1, w_qkv, w_vt, cos_t, sin_t, qg, kg, batch, seq, tm):
    d = x2.shape[1]
    tpb = seq // tm
    pair_shape = jax.ShapeDtypeStruct((batch, N_PAIRS, seq, LANES), BF16)
    pair_spec = pl.BlockSpec((1, N_PAIRS, tm, LANES), lambda i: (i // tpb, 0, i % tpb, 0))
    vec = lambda: pl.BlockSpec((1, 1, d), lambda i: (i // tpb, 0, 0))
    return pl.pallas_call(
        _proj_kernel,
        grid=(batch * tpb,),
        in_specs=[pl.BlockSpec((tm, d), lambda i: (i, 0)),
                  vec(), vec(),
                  pl.BlockSpec((1, d), lambda i: (0, 0)),
                  pl.BlockSpec(w_qkv.shape, lambda i: (0, 0)),
                  pl.BlockSpec(w_vt.shape, lambda i: (0, 0)),
                  pl.BlockSpec((tm, LANES), lambda i: (i % tpb, 0)),
                  pl.BlockSpec((tm, LANES), lambda i: (i % tpb, 0)),
                  pl.BlockSpec((1, LANES), lambda i: (0, 0)),
                  pl.BlockSpec((1, LANES), lambda i: (0, 0))],
        out_specs=[pair_spec, pair_spec, pair_spec, pair_spec,
                   pl.BlockSpec((1, tm, LANES), lambda i: (i // tpb, i % tpb, 0)),
                   pl.BlockSpec((1, LANES, tm), lambda i: (i // tpb, 0, i % tpb))],
        out_shape=[pair_shape, pair_shape, pair_shape, pair_shape,
                   jax.ShapeDtypeStruct((batch, seq, LANES), BF16),
                   jax.ShapeDtypeStruct((batch, LANES, seq), BF16)],
        compiler_params=_params("arbitrary"),
        name="proj",
    )(x2, shift1, scale1, g1, w_qkv, w_vt, cos_t, sin_t, qg, kg)


def _na_kernel(q_ref, k_ref, v_ref, bias_ref, o_ref, *, rows):
    band = NA_WIN_ROWS * GRID_W
    rows_per_trip = 8
    assert rows % rows_per_trip == 0
    lane = lax.broadcasted_iota(I32, (GRID_W, LANES), 1)
    low = lane < HEAD_DIM

    def body(rb, carry):
        staged = []
        for t in range(rows_per_trip):
            r = rb * rows_per_trip + t
            rs = jnp.clip(r - NA_WIN_ROWS // 2, 0, rows - NA_WIN_ROWS)
            lo = rs - r + (NA_WIN_ROWS - 1)
            q = q_ref[0, 0, pl.ds(pl.multiple_of(r * GRID_W, GRID_W), GRID_W), :]
            kb = k_ref[0, 0, pl.ds(pl.multiple_of(rs * GRID_W, GRID_W), band), :]
            zero = jnp.zeros_like(q)
            q2 = jnp.concatenate([jnp.where(low, q, zero), jnp.where(low, zero, q)], axis=0)
            staged.append((r, rs, _dot_nt(q2, kb) + bias_ref[0, lo]))
        for r, rs, s in staged:
            vb = v_ref[0, 0, pl.ds(pl.multiple_of(rs * GRID_W, GRID_W), band), :]
            m = jnp.max(s, axis=-1, keepdims=True)
            p = jnp.exp(s - m)
            l = jnp.sum(p, axis=-1, keepdims=True)
            o2 = _dot(p.astype(BF16), vb) / l
            o = jnp.where(low, o2[:GRID_W], o2[GRID_W:])
            o_ref[0, 0, pl.ds(pl.multiple_of(r * GRID_W, GRID_W), GRID_W), :] = o.astype(o_ref.dtype)
        return carry

    lax.fori_loop(0, rows // rows_per_trip, body, 0)


def _na(q, k, v, bias):
    batch, n_pairs, seq, _ = q.shape
    rows = seq // GRID_W
    spec = pl.BlockSpec((1, 1, seq, LANES), lambda b, j: (b, j, 0, 0))
    return pl.pallas_call(
        functools.partial(_na_kernel, rows=rows),
        grid=(batch, n_pairs),
        in_specs=[spec, spec, spec,
                  pl.BlockSpec((1,) + bias.shape[1:], lambda b, j: (j, 0, 0, 0))],
        out_specs=spec,
        out_shape=jax.ShapeDtypeStruct(q.shape, BF16),
        compiler_params=_params("arbitrary", "arbitrary"),
        name="na",
    )(q, k, v, bias)


def _na_bias_table(rpb):
    cols = np.arange(GRID_W)
    col_start = np.clip(cols - NA_WIN_COLS // 2, 0, GRID_W - NA_WIN_COLS)
    kc = np.arange(GRID_W)
    valid = (kc[None, :] >= col_start[:, None]) & (kc[None, :] < col_start[:, None] + NA_WIN_COLS)
    dc = np.clip(kc[None, :] - cols[:, None] + (NA_WIN_COLS - 1), 0, 2 * NA_WIN_COLS - 2)
    lo = np.arange(NA_WIN_ROWS)
    jb = np.arange(NA_WIN_ROWS)
    dr = lo[:, None] + jb[None, :]
    t = rpb[:, dr]
    t = t[:, :, :, dc]
    t = jnp.where(jnp.asarray(valid)[None, None, None], t, NEG)
    t = t.transpose(0, 1, 3, 2, 4).reshape(NA_HEADS, NA_WIN_ROWS, GRID_W, NA_WIN_ROWS * GRID_W)
    t = t.reshape(N_PAIRS, PAIR, NA_WIN_ROWS, GRID_W, NA_WIN_ROWS * GRID_W).transpose(0, 2, 1, 3, 4)
    return t.reshape(N_PAIRS, NA_WIN_ROWS, PAIR * GRID_W, NA_WIN_ROWS * GRID_W).astype(F32)


def _col_reduce(op, x, ways=64):
    rows, n = x.shape
    part = op(x.reshape(rows // ways, ways, n), axis=0)
    return op(part, axis=0, keepdims=True)


def _gqa_kernel(q_ref, k_ref, vt_ref, o_ref, *, kc):
    seq = k_ref.shape[1]
    tq = q_ref.shape[2]
    lane = lax.broadcasted_iota(I32, (tq, LANES), 1)
    low = lane < HEAD_DIM

    def masked_q(j):
        q = q_ref[0, j]
        zero = jnp.zeros_like(q)
        return jnp.concatenate([jnp.where(low, q, zero), jnp.where(low, zero, q)], axis=0)

    def scores(step):
        j, c0 = step
        return _dot_nt(k_ref[0, c0:c0 + kc, :], masked_q(j))

    steps = [(j, c0) for c0 in range(0, seq, kc) for j in range(N_PAIRS)]
    ahead = 2
    pending = [scores(s) for s in steps[:ahead]]
    state = {}
    for i, (j, c0) in enumerate(steps):
        st = pending.pop(0)
        if i + ahead < len(steps):
            pending.append(scores(steps[i + ahead]))
        if c0 == 0:
            state[j] = (jnp.full((1, 2 * tq), NEG, F32), jnp.zeros((1, 2 * tq), F32),
                        jnp.zeros((LANES, 2 * tq), F32))
        m, l, acc = state[j]
        m_new = jnp.maximum(m, _col_reduce(jnp.max, st))
        alpha = jnp.exp2(m - m_new)
        p = jnp.exp2(st - m_new)
        l = alpha * l + _col_reduce(jnp.sum, p)
        acc = alpha * acc + _dot(vt_ref[0, :, c0:c0 + kc], p.astype(BF16))
        state[j] = (m_new, l, acc)
        if c0 + kc == seq:
            ot = acc / l
            o_t = jnp.concatenate([ot[:HEAD_DIM, :tq], ot[HEAD_DIM:, tq:]], axis=0)
            o_ref[0, j] = o_t.T.astype(o_ref.dtype)


def _gqa(q, k, vt, tq, kc):
    batch, n_pairs, seq, _ = q.shape
    qspec = pl.BlockSpec((1, n_pairs, tq, LANES), lambda b, i: (b, 0, i, 0))
    return pl.pallas_call(
        functools.partial(_gqa_kernel, kc=kc),
        grid=(batch, seq // tq),
        in_specs=[qspec,
                  pl.BlockSpec((1, seq, LANES), lambda b, i: (b, 0, 0)),
                  pl.BlockSpec((1, LANES, seq), lambda b, i: (b, 0, 0))],
        out_specs=qspec,
        out_shape=jax.ShapeDtypeStruct(q.shape, BF16),
        compiler_params=_params("arbitrary", "arbitrary"),
        name="gqa",
    )(q, k, vt)


def _merge_kernel(x_ref, shift1_ref, scale1_ref, gate1_ref, shift2_ref, scale2_ref, g1_ref, g2_ref,
                  wgate_ref, yna_ref, ygq_ref, wna_ref, wgq_ref, wout_ref, wr_ref,
                  x1_ref, h2_ref, aff_ref):
    tm, d = x_ref.shape
    n_parts = 2
    rp = tm // n_parts
    parts = [slice(i * rp, (i + 1) * rp) for i in range(n_parts)]
    xs = [x_ref[r, :] for r in parts]
    hs = [((_rms(x) * g1_ref[...]) * (1.0 + scale1_ref[0]) + shift1_ref[0]).astype(BF16) for x in xs]
    logit_g = [_dot(h, wgate_ref[...]) for h in hs]
    ynas = [jnp.concatenate([yna_ref[0, j, r, :] for j in range(N_PAIRS)], axis=-1) for r in parts]
    ygqs = [jnp.concatenate([ygq_ref[0, j, r, :] for j in range(N_PAIRS)], axis=-1) for r in parts]
    br_na = [_dot(y, wna_ref[...]) for y in ynas]
    br_gq = [_dot(y, wgq_ref[...]) for y in ygqs]
    merged = []
    for lg, a, b in zip(logit_g, br_na, br_gq):
        gates = jax.nn.sigmoid(lg)
        merged.append((gates[:, :d] * a + gates[:, d:] * b).astype(BF16))
    outs = [_dot(mg, wout_ref[...]) for mg in merged]
    h2s = []
    for r, x, o in zip(parts, xs, outs):
        x1 = x + gate1_ref[0] * o
        x1_ref[r, :] = x1
        h2 = (_rms(x1) * g2_ref[...]) * (1.0 + scale2_ref[0]) + shift2_ref[0]
        h2_ref[r, :] = h2
        h2s.append(h2)
    for r, h2 in zip(parts, h2s):
        logits = _dot_nt(wr_ref[...], h2, lax.Precision.HIGHEST)
        z = jnp.exp(logits - jnp.max(logits, axis=0, keepdims=True))
        aff_ref[0, :, r] = z / jnp.sum(z, axis=0, keepdims=True)


def _merge(x2, mods, g1, g2, w_gate, yna, ygq, w_na, w_gq, w_out, w_router_t, batch, seq, tm):
    d = x2.shape[1]
    tpb = seq // tm
    n_exp = w_router_t.shape[0]
    vec = lambda: pl.BlockSpec((1, 1, d), lambda i: (i // tpb, 0, 0))
    full = lambda a: pl.BlockSpec(a.shape, lambda i: (0,) * a.ndim)
    row = pl.BlockSpec((tm, d), lambda i: (i, 0))
    pair_spec = pl.BlockSpec((1, N_PAIRS, tm, LANES), lambda i: (i // tpb, 0, i % tpb, 0))
    return pl.pallas_call(
        _merge_kernel,
        grid=(batch * tpb,),
        in_specs=[row, vec(), vec(), vec(), vec(), vec(), full(g1), full(g2), full(w_gate),
                  pair_spec, pair_spec, full(w_na), full(w_gq), full(w_out), full(w_router_t)],
        out_specs=[row, row, pl.BlockSpec((1, n_exp, tm), lambda i: (i // tpb, 0, i % tpb))],
        out_shape=[jax.ShapeDtypeStruct(x2.shape, F32), jax.ShapeDtypeStruct(x2.shape, F32),
                   jax.ShapeDtypeStruct((batch, n_exp, seq), F32)],
        compiler_params=_params("arbitrary"),
        name="merge",
    )(x2, *mods, g1, g2, w_gate, yna, ygq, w_na, w_gq, w_out, w_router_t)


def _topk_kernel(aff_ref, idx_ref, gate_ref, *, cap):
    n_exp, nblk, _ = aff_ref.shape[1:]
    n_bits = 31
    hi = lax.Precision.HIGHEST

    def search(it, ths):
        bit = jnp.left_shift(jnp.int32(1), n_bits - 1 - it)
        out = []
        for e in range(n_exp):
            bits = pltpu.bitcast(aff_ref[0, e], I32)
            cand = ths[e] | bit
            cnt = jnp.sum((bits >= cand).astype(F32), keepdims=True)
            out.append(jnp.where(cnt >= cap, cand, ths[e]))
        return tuple(out)

    ths = lax.fori_loop(0, n_bits, search, tuple(jnp.zeros((1, 1), I32) for _ in range(n_exp)))

    r_i = lax.broadcasted_iota(I32, (LANES, LANES), 0)
    c_i = lax.broadcasted_iota(I32, (LANES, LANES), 1)
    incl = (r_i <= c_i).astype(BF16)
    br = lax.broadcasted_iota(I32, (nblk, nblk), 0)
    bc = lax.broadcasted_iota(I32, (nblk, nblk), 1)
    strict_lower = (bc < br).astype(BF16)
    incl_blk = (br <= bc).astype(BF16)
    ones_rows = jnp.ones((SUBLANES, LANES), BF16)
    lane_vals = lax.broadcasted_iota(I32, (SUBLANES, LANES), 1).astype(BF16)
    blk_vals = lax.broadcasted_iota(I32, (SUBLANES, nblk), 1).astype(BF16)
    j_col = lax.broadcasted_iota(I32, (cap, nblk), 0).astype(F32)
    j_col_l = lax.broadcasted_iota(I32, (cap, LANES), 0).astype(F32)

    for e in range(n_exp):
        aff = aff_ref[0, e]
        bits = pltpu.bitcast(aff, I32)
        th = ths[e]
        gt = bits > th
        eq = bits == th
        need = cap - jnp.sum(gt.astype(F32), keepdims=True)
        eq_b = eq.astype(BF16)
        eq_cs = _dot(eq_b, incl)
        eq_tot = jnp.broadcast_to(eq_cs[:, LANES - 1:], (nblk, LANES)).astype(BF16)
        eq_rank = _dot(strict_lower, eq_tot) + eq_cs
        sel = gt | (eq & (eq_rank <= need))
        sel_b = sel.astype(BF16)
        cs = jnp.where(sel, _dot(sel_b, incl), 0.0)
        tot_row = _dot_nt(ones_rows, sel_b)
        cum_inc = _dot(tot_row.astype(BF16), incl_blk)[0:1]
        cum_exc = cum_inc - tot_row[0:1]
        onehot = (cum_exc <= j_col) & (j_col < cum_inc)
        onehot_b = onehot.astype(BF16)
        base = jnp.sum(jnp.where(onehot, cum_exc, 0.0), axis=-1, keepdims=True)
        local = j_col_l - base + 1.0
        g_cs = _dot(onehot_b, cs.astype(BF16))
        match = g_cs == local
        g_aff = _dot(onehot.astype(F32), aff, hi)
        blk_row = _dot_nt(blk_vals, onehot_b)
        lane_row = _dot_nt(lane_vals, match.astype(BF16))
        gate_row = _dot_nt(jnp.ones((SUBLANES, LANES), F32), jnp.where(match, g_aff, 0.0), hi)
        idx_ref[0, e:e + 1, :] = (blk_row[0:1] * LANES + lane_row[0:1]).astype(I32)
        gate_ref[0, e:e + 1, :] = gate_row[0:1]


def _topk(aff4, cap):
    batch, n_exp, nblk, _ = aff4.shape
    out_spec = pl.BlockSpec((1, n_exp, cap), lambda b: (b, 0, 0))
    return pl.pallas_call(
        functools.partial(_topk_kernel, cap=cap),
        grid=(batch,),
        in_specs=[pl.BlockSpec((1, n_exp, nblk, LANES), lambda b: (b, 0, 0, 0))],
        out_specs=[out_spec, out_spec],
        out_shape=[jax.ShapeDtypeStruct((batch, n_exp, cap), I32),
                   jax.ShapeDtypeStruct((batch, n_exp, cap), F32)],
        compiler_params=_params("arbitrary"),
        name="topk",
    )(aff4)


def _ffn_kernel(idx_ref, idx_next_ref, h2_ref, wg_ref, wu_ref, wd_ref, y_ref, xbuf, sem,
                *, n_exp, seq, row_chunk, ff_chunk):
    s = pl.program_id(0)
    n_steps = pl.num_programs(0)
    cap = xbuf.shape[1]
    ff = wg_ref.shape[2]
    slot = s % 2

    def row_copy(tok, i, slot_):
        return pltpu.make_async_copy(h2_ref.at[pl.ds(tok, 1), :], xbuf.at[slot_, pl.ds(i, 1), :], sem.at[slot_])

    def gather(ref, step, slot_):
        base = (step // n_exp) * seq

        def body(i, carry):
            row_copy(base + ref[0, 0, i], i, slot_).start()
            return carry

        lax.fori_loop(0, cap, body, 0, unroll=8)

    def wait_all(slot_):
        pltpu.make_async_copy(h2_ref.at[pl.ds(0, cap), :], xbuf.at[slot_], sem.at[slot_]).wait()

    @pl.when(s == 0)
    def _():
        gather(idx_ref, s, slot)

    wait_all(slot)

    nxt = jnp.minimum(s + 1, n_steps - 1)
    nxt_base = (nxt // n_exp) * seq
    groups = [(r0, f0) for r0 in range(0, cap, row_chunk) for f0 in range(0, ff, ff_chunk)]
    per_group = cap // len(groups)
    y = None
    for gi, (r0, f0) in enumerate(groups):
        for i in range(gi * per_group, (gi + 1) * per_group):
            row_copy(nxt_base + idx_next_ref[0, 0, i], i, 1 - slot).start()
        x = xbuf[slot, r0:r0 + row_chunk, :].astype(BF16)
        a = _dot(x, wg_ref[0, :, f0:f0 + ff_chunk])
        u = _dot(x, wu_ref[0, :, f0:f0 + ff_chunk])
        act = ((a * jax.nn.sigmoid(a)) * u).astype(BF16)
        part = _dot(act, wd_ref[0, f0:f0 + ff_chunk, :])
        y = part if f0 == 0 else y + part
        if f0 + ff_chunk == ff:
            y_ref[0, r0:r0 + row_chunk, :] = y

    @pl.when(s == n_steps - 1)
    def _():
        wait_all(1 - slot)


def _ffn(idx3, h2, wg, wu, wd, batch, seq):
    n_exp, d, ff = wg.shape
    cap = idx3.shape[2]
    n_steps = batch * n_exp
    kern = functools.partial(_ffn_kernel, n_exp=n_exp, seq=seq, row_chunk=min(512, cap), ff_chunk=min(1024, ff))
    smem = lambda f: pl.BlockSpec((1, 1, cap), f, memory_space=pltpu.SMEM)
    return pl.pallas_call(
        kern,
        grid=(n_steps,),
        in_specs=[smem(lambda s: (s, 0, 0)),
                  smem(lambda s: (jnp.minimum(s + 1, n_steps - 1), 0, 0)),
                  pl.BlockSpec(memory_space=pl.ANY),
                  pl.BlockSpec((1, d, ff), lambda s: (s % n_exp, 0, 0)),
                  pl.BlockSpec((1, d, ff), lambda s: (s % n_exp, 0, 0)),
                  pl.BlockSpec((1, ff, d), lambda s: (s % n_exp, 0, 0))],
        out_specs=pl.BlockSpec((1, cap, d), lambda s: (s, 0, 0)),
        out_shape=jax.ShapeDtypeStruct((n_steps, cap, d), F32),
        scratch_shapes=[pltpu.VMEM((2, cap, d), F32), pltpu.SemaphoreType.DMA((2,))],
        compiler_params=_params("arbitrary"),
        name="ffn",
    )(idx3, idx3, h2, wg, wu, wd)


def _combine_kernel(idx_ref, gate_ref, y_ref, gate2_ref, x1_ref, out_ref, acc, sem, *, n_exp, tc):
    b = pl.program_id(0)
    e = pl.program_id(1)
    c = pl.program_id(2)
    n_c = pl.num_programs(2)

    @pl.when((e == 0) & (c == 0))
    def _():
        cp = pltpu.make_async_copy(x1_ref.at[b], acc, sem.at[0])
        cp.start()
        cp.wait()

    gate2 = gate2_ref[0]
    sub = lax.broadcasted_iota(I32, (SUBLANES, acc.shape[1]), 0)

    def group(gi, carry):
        r0 = pl.multiple_of(gi * SUBLANES, SUBLANES)
        rows = y_ref[0, pl.ds(r0, SUBLANES), :] * gate2
        for r in range(SUBLANES):
            j = c * tc + r0 + r
            tok = idx_ref[0, 0, j]
            g = lax.bitcast_convert_type(gate_ref[0, 0, j], F32)
            base = pl.multiple_of((tok >> 3) << 3, SUBLANES)
            upd = jnp.where(sub == (tok & 7), rows[r:r + 1, :] * g, 0.0)
            acc[pl.ds(base, SUBLANES), :] = acc[pl.ds(base, SUBLANES), :] + upd
        return carry

    lax.fori_loop(0, tc // SUBLANES, group, 0)

    @pl.when((e == n_exp - 1) & (c == n_c - 1))
    def _():
        cp = pltpu.make_async_copy(acc, out_ref.at[b], sem.at[1])
        cp.start()
        cp.wait()


def _combine(idx3, gbits3, y, gate2, x1, batch, seq, tc):
    n_steps, cap, d = y.shape
    n_exp = n_steps // batch
    smem = lambda: pl.BlockSpec((1, 1, cap), lambda b, e, c: (b * n_exp + e, 0, 0), memory_space=pltpu.SMEM)
    return pl.pallas_call(
        functools.partial(_combine_kernel, n_exp=n_exp, tc=tc),
        grid=(batch, n_exp, cap // tc),
        in_specs=[smem(), smem(),
                  pl.BlockSpec((1, tc, d), lambda b, e, c: (b * n_exp + e, c, 0)),
                  pl.BlockSpec((1, 1, d), lambda b, e, c: (b, 0, 0)),
                  pl.BlockSpec(memory_space=pl.ANY)],
        out_specs=pl.BlockSpec(memory_space=pl.ANY),
        out_shape=jax.ShapeDtypeStruct((batch, seq, d), F32),
        scratch_shapes=[pltpu.VMEM((seq, d), F32), pltpu.SemaphoreType.DMA((2,))],
        compiler_params=_params("arbitrary", "arbitrary", "arbitrary"),
        name="combine",
    )(idx3, gbits3, y, gate2, x1)


def _final_kernel(x_ref, g_ref, o_ref):
    o_ref[...] = _rms(x_ref[...]) * g_ref[...]


def _final(x2, g, tm):
    n, d = x2.shape
    return pl.pallas_call(
        _final_kernel,
        grid=(n // tm,),
        in_specs=[pl.BlockSpec((tm, d), lambda i: (i, 0)), pl.BlockSpec((1, d), lambda i: (0, 0))],
        out_specs=pl.BlockSpec((tm, d), lambda i: (i, 0)),
        out_shape=jax.ShapeDtypeStruct((n, d), F32),
        compiler_params=_params("arbitrary"),
        name="final",
    )(x2, g)


def _rope_tables(seq):
    t = jnp.arange(seq)
    row = (t // GRID_W).astype(F32)
    col = (t % GRID_W).astype(F32)
    half = HEAD_DIM // 2
    inv_freq = ROPE_BASE ** (-jnp.arange(0, half, 2, dtype=F32) / half)
    ang = jnp.concatenate([row[:, None] * inv_freq[None], col[:, None] * inv_freq[None]], axis=-1)
    cos = jnp.repeat(jnp.cos(ang), 2, axis=-1)
    sin = jnp.repeat(jnp.sin(ang), 2, axis=-1) * jnp.tile(jnp.array([-1.0, 1.0], F32), half)
    return jnp.tile(cos, (1, PAIR)), jnp.tile(sin, (1, PAIR))


def _pair_gqa_heads(w, axis):
    group = GQA_HEADS // GQA_KV_HEADS
    shape = w.shape
    w = w.reshape(shape[:axis] + (GQA_KV_HEADS, group, HEAD_DIM) + shape[axis + 1:])
    return jnp.swapaxes(w, axis, axis + 1).reshape(shape)


def kernel(x, c, w_ada, b_ada, norm1_g, w_in, q_norm_g, k_norm_g, na_rpb, w_branch_na, w_branch_gqa,
           w_out, norm2_g, w_router, w_exp_gate, w_exp_up, w_exp_down, final_g):
    batch, seq, d = x.shape
    depth = w_ada.shape[0]
    n_exp = w_router.shape[2]
    cap = EC_CAPACITY_FACTOR * seq // n_exp
    na_w = NA_HEADS * HEAD_DIM
    gq_w = GQA_HEADS * HEAD_DIM
    kv_w = GQA_KV_HEADS * HEAD_DIM
    assert kv_w == LANES and GQA_HEADS // GQA_KV_HEADS == N_PAIRS and seq % LANES == 0

    cos_t, sin_t = _rope_tables(seq)
    x2 = x.reshape(batch * seq, d)
    c_pad = jnp.zeros((SUBLANES, d), F32).at[:batch].set(c)

    for l in range(depth):
        mod = _mod(c_pad, w_ada[l], b_ada[l][None, :])[:batch]
        shift1, scale1, gate1, shift2, scale2, gate2 = [m[:, None, :] for m in jnp.split(mod, 6, axis=-1)]

        w = w_in[l]
        q0 = 3 * na_w
        w_qkv = jnp.concatenate([w[:, :q0], _pair_gqa_heads(w[:, q0:q0 + gq_w], 1),
                                 w[:, q0 + gq_w:q0 + gq_w + kv_w]], axis=1).astype(BF16)
        w_vt = w[:, q0 + gq_w + kv_w:q0 + gq_w + 2 * kv_w].T.astype(BF16)
        w_gate = w[:, q0 + gq_w + 2 * kv_w:].astype(BF16)
        qg = jnp.tile(q_norm_g[l], PAIR)[None, :]
        kg = jnp.tile(k_norm_g[l], PAIR)[None, :]
        g1 = norm1_g[l][None, :]
        g2 = norm2_g[l][None, :]

        qna, kna, vna, qgq, kgq, vgqt = _proj(x2, shift1, scale1, g1, w_qkv, w_vt, cos_t, sin_t, qg, kg,
                                              batch, seq, tm=512)
        yna = _na(qna, kna, vna, _na_bias_table(na_rpb[l]))
        ygq = _gqa(qgq, kgq, vgqt, tq=256, kc=min(512, seq))

        x1, h2, aff = _merge(x2, (shift1, scale1, gate1, shift2, scale2), g1, g2, w_gate, yna, ygq,
                             w_branch_na[l].astype(BF16), _pair_gqa_heads(w_branch_gqa[l], 0).astype(BF16),
                             w_out[l].astype(BF16), w_router[l].T, batch, seq, tm=512)

        idx, gates = _topk(aff.reshape(batch, n_exp, seq // LANES, LANES), cap)
        idx3 = idx.reshape(batch * n_exp, 1, cap)
        gbits3 = lax.bitcast_convert_type(gates, I32).reshape(batch * n_exp, 1, cap)
        y = _ffn(idx3, h2, w_exp_gate[l].astype(BF16), w_exp_up[l].astype(BF16), w_exp_down[l].astype(BF16),
                 batch, seq)
        x2 = _combine(idx3, gbits3, y, gate2, x1.reshape(batch, seq, d), batch, seq,
                      tc=min(256, cap)).reshape(batch * seq, d)

    return _final(x2, final_g[None, :], tm=1024).reshape(batch, seq, d)
```

```python
import functools

import numpy as np
import jax
import jax.numpy as jnp
from jax import lax
from jax.experimental import pallas as pl
from jax.experimental.pallas import tpu as pltpu

F32 = jnp.float32
BF16 = jnp.bfloat16
I32 = jnp.int32

GRID_W = 64
HEAD_DIM = 64
NA_HEADS = 8
NA_WIN_ROWS = 8
NA_WIN_COLS = 16
GQA_HEADS = 8
GQA_KV_HEADS = 2
ROPE_BASE = 10000.0
N_EXPERTS = 16
EC_CAPACITY_FACTOR = 2
EPS = 1e-6

LANES = 128
SUBLANES = 8
PAIR = LANES // HEAD_DIM
N_PAIRS = NA_HEADS // PAIR
NEG = -1e30
LOG2E = 1.4426950408889634
VMEM_LIMIT = 56 * 1024 * 1024

_NT = (((1,), (1,)), ((), ()))


def _dot(a, b, precision=None):
    return jnp.dot(a, b, preferred_element_type=F32, precision=precision)


def _dot_nt(a, b, precision=None):
    return lax.dot_general(a, b, _NT, preferred_element_type=F32, precision=precision)


def _rms(x):
    return x * lax.rsqrt(jnp.mean(x * x, axis=-1, keepdims=True) + EPS)


def _params(*sem):
    return pltpu.CompilerParams(dimension_semantics=sem, vmem_limit_bytes=VMEM_LIMIT)


def _mod_kernel(c_ref, w_ref, b_ref, o_ref):
    c = c_ref[...]
    sc = c * jax.nn.sigmoid(c)
    o_ref[...] = _dot(sc, w_ref[...], lax.Precision.HIGHEST) + b_ref[...]


def _mod(c_pad, w_ada, b_ada):
    rows, d = c_pad.shape
    n = w_ada.shape[1]
    tn = 1024
    return pl.pallas_call(
        _mod_kernel,
        grid=(n // tn,),
        in_specs=[pl.BlockSpec((rows, d), lambda j: (0, 0)),
                  pl.BlockSpec((d, tn), lambda j: (0, j)),
                  pl.BlockSpec((1, tn), lambda j: (0, j))],
        out_specs=pl.BlockSpec((rows, tn), lambda j: (0, j)),
        out_shape=jax.ShapeDtypeStruct((rows, n), F32),
        compiler_params=_params("arbitrary"),
        name="mod",
    )(c_pad, w_ada, b_ada)


def _head_rms_rope(x, gain, cos, sin_signed):
    lane = lax.broadcasted_iota(I32, x.shape, 1)
    low = lane < HEAD_DIM
    ss = x * x
    s_lo = jnp.sum(jnp.where(low, ss, 0.0), axis=-1, keepdims=True)
    s_hi = jnp.sum(jnp.where(low, 0.0, ss), axis=-1, keepdims=True)
    r = jnp.where(low, lax.rsqrt(s_lo / HEAD_DIM + EPS), lax.rsqrt(s_hi / HEAD_DIM + EPS))
    y = (x * r) * gain
    nxt = pltpu.roll(y, LANES - 1, 1)
    prv = pltpu.roll(y, 1, 1)
    partner = jnp.where(lane % 2 == 0, nxt, prv)
    return y * cos + partner * sin_signed


def _proj_kernel(x_ref, shift_ref, scale_ref, g_ref, w_ref, wvt_ref, cos_ref, sin_ref, qg_ref, kg_ref,
                 qna_ref, kna_ref, vna_ref, qgq_ref, kgq_ref, vgqt_ref):
    tm = x_ref.shape[0]
    n_parts = 2
    rp = tm // n_parts
    parts = [slice(i * rp, (i + 1) * rp) for i in range(n_parts)]
    hs = [((_rms(x_ref[r, :]) * g_ref[...]) * (1.0 + scale_ref[0]) + shift_ref[0]).astype(BF16) for r in parts]
    projs = [_dot(h, w_ref[...]) for h in hs]
    for r, h in zip(parts, hs):
        vgqt_ref[0, :, r] = _dot_nt(wvt_ref[...], h).astype(BF16)
    scale = HEAD_DIM ** -0.5
    na_w = N_PAIRS * LANES
    base = 3 * na_w
    for r, proj in zip(parts, projs):
        for j in range(N_PAIRS):
            c0 = j * LANES
            qna_ref[0, j, r, :] = (proj[:, c0:c0 + LANES] * scale).astype(BF16)
            kna_ref[0, j, r, :] = proj[:, na_w + c0:na_w + c0 + LANES].astype(BF16)
            vna_ref[0, j, r, :] = proj[:, 2 * na_w + c0:2 * na_w + c0 + LANES].astype(BF16)
        cos = cos_ref[r, :]
        sin = sin_ref[r, :]
        for j in range(N_PAIRS):
            c0 = base + j * LANES
            q = _head_rms_rope(proj[:, c0:c0 + LANES], qg_ref[...], cos, sin)
            qgq_ref[0, j, r, :] = (q * (scale * LOG2E)).astype(BF16)
        c0 = base + N_PAIRS * LANES
        kgq_ref[0, r, :] = _head_rms_rope(proj[:, c0:c0 + LANES], kg_ref[...], cos, sin).astype(BF16)


def _proj(x2, shift1, scale1, g1, w_qkv, w_vt, cos_t, sin_t, qg, kg, batch, seq, tm):
    d = x2.shape[1]
    tpb = seq // tm
    pair_shape = jax.ShapeDtypeStruct((batch, N_PAIRS, seq, LANES), BF16)
    pair_spec = pl.BlockSpec((1, N_PAIRS, tm, LANES), lambda i: (i // tpb, 0, i % tpb, 0))
    vec = lambda: pl.BlockSpec((1, 1, d), lambda i: (i // tpb, 0, 0))
    return pl.pallas_call(
        _proj_kernel,
        grid=(batch * tpb,),
        in_specs=[pl.BlockSpec((tm, d), lambda i: (i, 0)),
                  vec(), vec(),
                  pl.BlockSpec((1, d), lambda i: (0, 0)),
                  pl.BlockSpec(w_qkv.shape, lambda i: (0, 0)),
                  pl.BlockSpec(w_vt.shape, lambda i: (0, 0)),
                  pl.BlockSpec((tm, LANES), lambda i: (i % tpb, 0)),
                  pl.BlockSpec((tm, LANES), lambda i: (i % tpb, 0)),
                  pl.BlockSpec((1, LANES), lambda i: (0, 0)),
                  pl.BlockSpec((1, LANES), lambda i: (0, 0))],
        out_specs=[pair_spec, pair_spec, pair_spec, pair_spec,
                   pl.BlockSpec((1, tm, LANES), lambda i: (i // tpb, i % tpb, 0)),
                   pl.BlockSpec((1, LANES, tm), lambda i: (i // tpb, 0, i % tpb))],
        out_shape=[pair_shape, pair_shape, pair_shape, pair_shape,
                   jax.ShapeDtypeStruct((batch, seq, LANES), BF16),
                   jax.ShapeDtypeStruct((batch, LANES, seq), BF16)],
        compiler_params=_params("arbitrary"),
        name="proj",
    )(x2, shift1, scale1, g1, w_qkv, w_vt, cos_t, sin_t, qg, kg)


def _na_kernel(q_ref, k_ref, v_ref, bias_ref, o_ref, *, rows):
    band = NA_WIN_ROWS * GRID_W
    rows_per_trip = 8
    assert rows % rows_per_trip == 0
    lane = lax.broadcasted_iota(I32, (GRID_W, LANES), 1)
    low = lane < HEAD_DIM

    def body(rb, carry):
        staged = []
        for t in range(rows_per_trip):
            r = rb * rows_per_trip + t
            rs = jnp.clip(r - NA_WIN_ROWS // 2, 0, rows - NA_WIN_ROWS)
            lo = rs - r + (NA_WIN_ROWS - 1)
            q = q_ref[0, 0, pl.ds(pl.multiple_of(r * GRID_W, GRID_W), GRID_W), :]
            kb = k_ref[0, 0, pl.ds(pl.multiple_of(rs * GRID_W, GRID_W), band), :]
            zero = jnp.zeros_like(q)
            q2 = jnp.concatenate([jnp.where(low, q, zero), jnp.where(low, zero, q)], axis=0)
            staged.append((r, rs, _dot_nt(q2, kb) + bias_ref[0, lo]))
        for r, rs, s in staged:
            vb = v_ref[0, 0, pl.ds(pl.multiple_of(rs * GRID_W, GRID_W), band), :]
            m = jnp.max(s, axis=-1, keepdims=True)
            p = jnp.exp(s - m)
            l = jnp.sum(p, axis=-1, keepdims=True)
            o2 = _dot(p.astype(BF16), vb) / l
            o = jnp.where(low, o2[:GRID_W], o2[GRID_W:])
            o_ref[0, 0, pl.ds(pl.multiple_of(r * GRID_W, GRID_W), GRID_W), :] = o.astype(o_ref.dtype)
        return carry

    lax.fori_loop(0, rows // rows_per_trip, body, 0)


def _na(q, k, v, bias):
    batch, n_pairs, seq, _ = q.shape
    rows = seq // GRID_W
    spec = pl.BlockSpec((1, 1, seq, LANES), lambda b, j: (b, j, 0, 0))
    return pl.pallas_call(
        functools.partial(_na_kernel, rows=rows),
        grid=(batch, n_pairs),
        in_specs=[spec, spec, spec,
                  pl.BlockSpec((1,) + bias.shape[1:], lambda b, j: (j, 0, 0, 0))],
        out_specs=spec,
        out_shape=jax.ShapeDtypeStruct(q.shape, BF16),
        compiler_params=_params("arbitrary", "arbitrary"),
        name="na",
    )(q, k, v, bias)


def _na_bias_table(rpb):
    cols = np.arange(GRID_W)
    col_start = np.clip(cols - NA_WIN_COLS // 2, 0, GRID_W - NA_WIN_COLS)
    kc = np.arange(GRID_W)
    valid = (kc[None, :] >= col_start[:, None]) & (kc[None, :] < col_start[:, None] + NA_WIN_COLS)
    dc = np.clip(kc[None, :] - cols[:, None] + (NA_WIN_COLS - 1), 0, 2 * NA_WIN_COLS - 2)
    lo = np.arange(NA_WIN_ROWS)
    jb = np.arange(NA_WIN_ROWS)
    dr = lo[:, None] + jb[None, :]
    n_r, n_c = NA_WIN_ROWS * NA_WIN_ROWS, GRID_W * GRID_W
    sel_r = np.zeros((n_r, 2 * NA_WIN_ROWS - 1), np.float32)
    sel_r[np.arange(n_r), dr.reshape(-1)] = 1.0
    sel_c = np.zeros((2 * NA_WIN_COLS - 1, n_c), np.float32)
    sel_c[dc.reshape(-1), np.arange(n_c)] = 1.0
    t = jnp.einsum('ad,hdr,rb->hab', sel_r, rpb, sel_c, precision=lax.Precision.HIGHEST)
    t = t.reshape(NA_HEADS, NA_WIN_ROWS, NA_WIN_ROWS, GRID_W, GRID_W)
    t = jnp.where(jnp.asarray(valid)[None, None, None], t, NEG)
    t = t.transpose(0, 1, 3, 2, 4).reshape(NA_HEADS, NA_WIN_ROWS, GRID_W, NA_WIN_ROWS * GRID_W)
    t = t.reshape(N_PAIRS, PAIR, NA_WIN_ROWS, GRID_W, NA_WIN_ROWS * GRID_W).transpose(0, 2, 1, 3, 4)
    return t.reshape(N_PAIRS, NA_WIN_ROWS, PAIR * GRID_W, NA_WIN_ROWS * GRID_W).astype(F32)


def _col_reduce(op, x, ways=64):
    rows, n = x.shape
    part = op(x.reshape(rows // ways, ways, n), axis=0)
    return op(part, axis=0, keepdims=True)


def _gqa_kernel(q_ref, k_ref, vt_ref, o_ref, *, kc):
    seq = k_ref.shape[1]
    tq = q_ref.shape[2]
    lane = lax.broadcasted_iota(I32, (tq, LANES), 1)
    low = lane < HEAD_DIM

    def masked_q(j):
        q = q_ref[0, j]
        zero = jnp.zeros_like(q)
        return jnp.concatenate([jnp.where(low, q, zero), jnp.where(low, zero, q)], axis=0)

    def scores(step):
        j, c0 = step
        return _dot_nt(k_ref[0, c0:c0 + kc, :], masked_q(j))

    steps = [(j, c0) for c0 in range(0, seq, kc) for j in range(N_PAIRS)]
    ahead = 2
    pending = [scores(s) for s in steps[:ahead]]
    state = {}
    for i, (j, c0) in enumerate(steps):
        st = pending.pop(0)
        if i + ahead < len(steps):
            pending.append(scores(steps[i + ahead]))
        if c0 == 0:
            state[j] = (jnp.full((1, 2 * tq), NEG, F32), jnp.zeros((1, 2 * tq), F32),
                        jnp.zeros((LANES, 2 * tq), F32))
        m, l, acc = state[j]
        m_new = jnp.maximum(m, _col_reduce(jnp.max, st))
        alpha = jnp.exp2(m - m_new)
        p = jnp.exp2(st - m_new)
        l = alpha * l + _col_reduce(jnp.sum, p)
        acc = alpha * acc + _dot(vt_ref[0, :, c0:c0 + kc], p.astype(BF16))
        state[j] = (m_new, l, acc)
        if c0 + kc == seq:
            ot = acc / l
            o_t = jnp.concatenate([ot[:HEAD_DIM, :tq], ot[HEAD_DIM:, tq:]], axis=0)
            o_ref[0, j] = o_t.T.astype(o_ref.dtype)


def _gqa(q, k, vt, tq, kc):
    batch, n_pairs, seq, _ = q.shape
    qspec = pl.BlockSpec((1, n_pairs, tq, LANES), lambda b, i: (b, 0, i, 0))
    return pl.pallas_call(
        functools.partial(_gqa_kernel, kc=kc),
        grid=(batch, seq // tq),
        in_specs=[qspec,
                  pl.BlockSpec((1, seq, LANES), lambda b, i: (b, 0, 0)),
                  pl.BlockSpec((1, LANES, seq), lambda b, i: (b, 0, 0))],
        out_specs=qspec,
        out_shape=jax.ShapeDtypeStruct(q.shape, BF16),
        compiler_params=_params("arbitrary", "arbitrary"),
        name="gqa",
    )(q, k, vt)


def _merge_kernel(x_ref, shift1_ref, scale1_ref, gate1_ref, shift2_ref, scale2_ref, g1_ref, g2_ref,
                  wgate_ref, yna_ref, ygq_ref, wna_ref, wgq_ref, wout_ref, wr_ref,
                  x1_ref, h2_ref, aff_ref):
    tm, d = x_ref.shape
    n_parts = 2
    rp = tm // n_parts
    parts = [slice(i * rp, (i + 1) * rp) for i in range(n_parts)]
    xs = [x_ref[r, :] for r in parts]
    hs = [((_rms(x) * g1_ref[...]) * (1.0 + scale1_ref[0]) + shift1_ref[0]).astype(BF16) for x in xs]
    logit_g = [_dot(h, wgate_ref[...]) for h in hs]
    ynas = [jnp.concatenate([yna_ref[0, j, r, :] for j in range(N_PAIRS)], axis=-1) for r in parts]
    ygqs = [jnp.concatenate([ygq_ref[0, j, r, :] for j in range(N_PAIRS)], axis=-1) for r in parts]
    br_na = [_dot(y, wna_ref[...]) for y in ynas]
    br_gq = [_dot(y, wgq_ref[...]) for y in ygqs]
    merged = []
    for lg, a, b in zip(logit_g, br_na, br_gq):
        gates = jax.nn.sigmoid(lg)
        merged.append((gates[:, :d] * a + gates[:, d:] * b).astype(BF16))
    outs = [_dot(mg, wout_ref[...]) for mg in merged]
    h2s = []
    for r, x, o in zip(parts, xs, outs):
        x1 = x + gate1_ref[0] * o
        x1_ref[r, :] = x1
        h2 = (_rms(x1) * g2_ref[...]) * (1.0 + scale2_ref[0]) + shift2_ref[0]
        h2_ref[r, :] = h2
        h2s.append(h2)
    for r, h2 in zip(parts, h2s):
        logits = _dot_nt(wr_ref[...], h2, lax.Precision.HIGHEST)
        z = jnp.exp(logits - jnp.max(logits, axis=0, keepdims=True))
        aff_ref[0, :, r] = z / jnp.sum(z, axis=0, keepdims=True)


def _merge(x2, mods, g1, g2, w_gate, yna, ygq, w_na, w_gq, w_out, w_router_t, batch, seq, tm):
    d = x2.shape[1]
    tpb = seq // tm
    n_exp = w_router_t.shape[0]
    vec = lambda: pl.BlockSpec((1, 1, d), lambda i: (i // tpb, 0, 0))
    full = lambda a: pl.BlockSpec(a.shape, lambda i: (0,) * a.ndim)
    row = pl.BlockSpec((tm, d), lambda i: (i, 0))
    pair_spec = pl.BlockSpec((1, N_PAIRS, tm, LANES), lambda i: (i // tpb, 0, i % tpb, 0))
    return pl.pallas_call(
        _merge_kernel,
        grid=(batch * tpb,),
        in_specs=[row, vec(), vec(), vec(), vec(), vec(), full(g1), full(g2), full(w_gate),
                  pair_spec, pair_spec, full(w_na), full(w_gq), full(w_out), full(w_router_t)],
        out_specs=[row, row, pl.BlockSpec((1, n_exp, tm), lambda i: (i // tpb, 0, i % tpb))],
        out_shape=[jax.ShapeDtypeStruct(x2.shape, F32), jax.ShapeDtypeStruct(x2.shape, F32),
                   jax.ShapeDtypeStruct((batch, n_exp, seq), F32)],
        compiler_params=_params("arbitrary"),
        name="merge",
    )(x2, *mods, g1, g2, w_gate, yna, ygq, w_na, w_gq, w_out, w_router_t)


def _topk_kernel(aff_ref, idx_ref, gate_ref, *, cap):
    n_exp, nblk, _ = aff_ref.shape[1:]
    n_bits = 31
    hi = lax.Precision.HIGHEST

    def search(it, ths):
        bit = jnp.left_shift(jnp.int32(1), n_bits - 1 - it)
        out = []
        for e in range(n_exp):
            bits = pltpu.bitcast(aff_ref[0, e], I32)
            cand = ths[e] | bit
            cnt = jnp.sum((bits >= cand).astype(F32), keepdims=True)
            out.append(jnp.where(cnt >= cap, cand, ths[e]))
        return tuple(out)

    ths = lax.fori_loop(0, n_bits, search, tuple(jnp.zeros((1, 1), I32) for _ in range(n_exp)))

    r_i = lax.broadcasted_iota(I32, (LANES, LANES), 0)
    c_i = lax.broadcasted_iota(I32, (LANES, LANES), 1)
    incl = (r_i <= c_i).astype(BF16)
    br = lax.broadcasted_iota(I32, (nblk, nblk), 0)
    bc = lax.broadcasted_iota(I32, (nblk, nblk), 1)
    strict_lower = (bc < br).astype(BF16)
    incl_blk = (br <= bc).astype(BF16)
    ones_rows = jnp.ones((SUBLANES, LANES), BF16)
    lane_vals = lax.broadcasted_iota(I32, (SUBLANES, LANES), 1).astype(BF16)
    blk_vals = lax.broadcasted_iota(I32, (SUBLANES, nblk), 1).astype(BF16)
    j_col = lax.broadcasted_iota(I32, (cap, nblk), 0).astype(F32)
    j_col_l = lax.broadcasted_iota(I32, (cap, LANES), 0).astype(F32)

    for e in range(n_exp):
        aff = aff_ref[0, e]
        bits = pltpu.bitcast(aff, I32)
        th = ths[e]
        gt = bits > th
        eq = bits == th
        need = cap - jnp.sum(gt.astype(F32), keepdims=True)
        eq_b = eq.astype(BF16)
        eq_cs = _dot(eq_b, incl)
        eq_tot = jnp.broadcast_to(eq_cs[:, LANES - 1:], (nblk, LANES)).astype(BF16)
        eq_rank = _dot(strict_lower, eq_tot) + eq_cs
        sel = gt | (eq & (eq_rank <= need))
        sel_b = sel.astype(BF16)
        cs = jnp.where(sel, _dot(sel_b, incl), 0.0)
        tot_row = _dot_nt(ones_rows, sel_b)
        cum_inc = _dot(tot_row.astype(BF16), incl_blk)[0:1]
        cum_exc = cum_inc - tot_row[0:1]
        onehot = (cum_exc <= j_col) & (j_col < cum_inc)
        onehot_b = onehot.astype(BF16)
        base = jnp.sum(jnp.where(onehot, cum_exc, 0.0), axis=-1, keepdims=True)
        local = j_col_l - base + 1.0
        g_cs = _dot(onehot_b, cs.astype(BF16))
        match = g_cs == local
        g_aff = _dot(onehot.astype(F32), aff, hi)
        blk_row = _dot_nt(blk_vals, onehot_b)
        lane_row = _dot_nt(lane_vals, match.astype(BF16))
        gate_row = _dot_nt(jnp.ones((SUBLANES, LANES), F32), jnp.where(match, g_aff, 0.0), hi)
        idx_ref[0, e:e + 1, :] = (blk_row[0:1] * LANES + lane_row[0:1]).astype(I32)
        gate_ref[0, e:e + 1, :] = gate_row[0:1]


def _topk(aff4, cap):
    batch, n_exp, nblk, _ = aff4.shape
    out_spec = pl.BlockSpec((1, n_exp, cap), lambda b: (b, 0, 0))
    return pl.pallas_call(
        functools.partial(_topk_kernel, cap=cap),
        grid=(batch,),
        in_specs=[pl.BlockSpec((1, n_exp, nblk, LANES), lambda b: (b, 0, 0, 0))],
        out_specs=[out_spec, out_spec],
        out_shape=[jax.ShapeDtypeStruct((batch, n_exp, cap), I32),
                   jax.ShapeDtypeStruct((batch, n_exp, cap), F32)],
        compiler_params=_params("arbitrary"),
        name="topk",
    )(aff4)


def _ffn_kernel(idx_ref, idx_next_ref, h2_ref, wg_ref, wu_ref, wd_ref, y_ref, xbuf, sem,
                *, n_exp, seq, row_chunk, ff_chunk):
    s = pl.program_id(0)
    n_steps = pl.num_programs(0)
    cap = xbuf.shape[1]
    ff = wg_ref.shape[2]
    slot = s % 2

    def row_copy(tok, i, slot_):
        return pltpu.make_async_copy(h2_ref.at[pl.ds(tok, 1), :], xbuf.at[slot_, pl.ds(i, 1), :], sem.at[slot_])

    def gather(ref, step, slot_):
        base = (step // n_exp) * seq

        def body(i, carry):
            row_copy(base + ref[0, 0, i], i, slot_).start()
            return carry

        lax.fori_loop(0, cap, body, 0, unroll=8)

    def wait_all(slot_):
        pltpu.make_async_copy(h2_ref.at[pl.ds(0, cap), :], xbuf.at[slot_], sem.at[slot_]).wait()

    @pl.when(s == 0)
    def _():
        gather(idx_ref, s, slot)

    wait_all(slot)

    nxt = jnp.minimum(s + 1, n_steps - 1)
    nxt_base = (nxt // n_exp) * seq
    groups = [(r0, f0) for r0 in range(0, cap, row_chunk) for f0 in range(0, ff, ff_chunk)]
    per_group = cap // len(groups)
    y = None
    for gi, (r0, f0) in enumerate(groups):
        for i in range(gi * per_group, (gi + 1) * per_group):
            row_copy(nxt_base + idx_next_ref[0, 0, i], i, 1 - slot).start()
        x = xbuf[slot, r0:r0 + row_chunk, :].astype(BF16)
        a = _dot(x, wg_ref[0, :, f0:f0 + ff_chunk])
        u = _dot(x, wu_ref[0, :, f0:f0 + ff_chunk])
        act = ((a * jax.nn.sigmoid(a)) * u).astype(BF16)
        part = _dot(act, wd_ref[0, f0:f0 + ff_chunk, :])
        y = part if f0 == 0 else y + part
        if f0 + ff_chunk == ff:
            y_ref[0, r0:r0 + row_chunk, :] = y

    @pl.when(s == n_steps - 1)
    def _():
        wait_all(1 - slot)


def _ffn(idx3, h2, wg, wu, wd, batch, seq):
    n_exp, d, ff = wg.shape
    cap = idx3.shape[2]
    n_steps = batch * n_exp
    kern = functools.partial(_ffn_kernel, n_exp=n_exp, seq=seq, row_chunk=min(512, cap), ff_chunk=min(1024, ff))
    smem = lambda f: pl.BlockSpec((1, 1, cap), f, memory_space=pltpu.SMEM)
    return pl.pallas_call(
        kern,
        grid=(n_steps,),
        in_specs=[smem(lambda s: (s, 0, 0)),
                  smem(lambda s: (jnp.minimum(s + 1, n_steps - 1), 0, 0)),
                  pl.BlockSpec(memory_space=pl.ANY),
                  pl.BlockSpec((1, d, ff), lambda s: (s % n_exp, 0, 0)),
                  pl.BlockSpec((1, d, ff), lambda s: (s % n_exp, 0, 0)),
                  pl.BlockSpec((1, ff, d), lambda s: (s % n_exp, 0, 0))],
        out_specs=pl.BlockSpec((1, cap, d), lambda s: (s, 0, 0)),
        out_shape=jax.ShapeDtypeStruct((n_steps, cap, d), F32),
        scratch_shapes=[pltpu.VMEM((2, cap, d), F32), pltpu.SemaphoreType.DMA((2,))],
        compiler_params=_params("arbitrary"),
        name="ffn",
    )(idx3, idx3, h2, wg, wu, wd)


def _combine_kernel(idx_ref, gate_ref, y_ref, gate2_ref, x1_ref, out_ref, acc, sem, *, n_exp, tc):
    b = pl.program_id(0)
    e = pl.program_id(1)
    c = pl.program_id(2)
    n_c = pl.num_programs(2)

    @pl.when((e == 0) & (c == 0))
    def _():
        cp = pltpu.make_async_copy(x1_ref.at[b], acc, sem.at[0])
        cp.start()
        cp.wait()

    gate2 = gate2_ref[0]
    sub = lax.broadcasted_iota(I32, (SUBLANES, acc.shape[1]), 0)

    def group(gi, carry):
        r0 = pl.multiple_of(gi * SUBLANES, SUBLANES)
        rows = y_ref[0, pl.ds(r0, SUBLANES), :] * gate2
        for r in range(SUBLANES):
            j = c * tc + r0 + r
            tok = idx_ref[0, 0, j]
            g = lax.bitcast_convert_type(gate_ref[0, 0, j], F32)
            base = pl.multiple_of((tok >> 3) << 3, SUBLANES)
            upd = jnp.where(sub == (tok & 7), rows[r:r + 1, :] * g, 0.0)
            acc[pl.ds(base, SUBLANES), :] = acc[pl.ds(base, SUBLANES), :] + upd
        return carry

    lax.fori_loop(0, tc // SUBLANES, group, 0)

    @pl.when((e == n_exp - 1) & (c == n_c - 1))
    def _():
        cp = pltpu.make_async_copy(acc, out_ref.at[b], sem.at[1])
        cp.start()
        cp.wait()


def _combine(idx3, gbits3, y, gate2, x1, batch, seq, tc):
    n_steps, cap, d = y.shape
    n_exp = n_steps // batch
    smem = lambda: pl.BlockSpec((1, 1, cap), lambda b, e, c: (b * n_exp + e, 0, 0), memory_space=pltpu.SMEM)
    return pl.pallas_call(
        functools.partial(_combine_kernel, n_exp=n_exp, tc=tc),
        grid=(batch, n_exp, cap // tc),
        in_specs=[smem(), smem(),
                  pl.BlockSpec((1, tc, d), lambda b, e, c: (b * n_exp + e, c, 0)),
                  pl.BlockSpec((1, 1, d), lambda b, e, c: (b, 0, 0)),
                  pl.BlockSpec(memory_space=pl.ANY)],
        out_specs=pl.BlockSpec(memory_space=pl.ANY),
        out_shape=jax.ShapeDtypeStruct((batch, seq, d), F32),
        scratch_shapes=[pltpu.VMEM((seq, d), F32), pltpu.SemaphoreType.DMA((2,))],
        compiler_params=_params("arbitrary", "arbitrary", "arbitrary"),
        name="combine",
    )(idx3, gbits3, y, gate2, x1)


def _final_kernel(x_ref, g_ref, o_ref):
    o_ref[...] = _rms(x_ref[...]) * g_ref[...]


def _final(x2, g, tm):
    n, d = x2.shape
    return pl.pallas_call(
        _final_kernel,
        grid=(n // tm,),
        in_specs=[pl.BlockSpec((tm, d), lambda i: (i, 0)), pl.BlockSpec((1, d), lambda i: (0, 0))],
        out_specs=pl.BlockSpec((tm, d), lambda i: (i, 0)),
        out_shape=jax.ShapeDtypeStruct((n, d), F32),
        compiler_params=_params("arbitrary"),
        name="final",
    )(x2, g)


def _rope_tables(seq):
    t = jnp.arange(seq)
    row = (t // GRID_W).astype(F32)
    col = (t % GRID_W).astype(F32)
    half = HEAD_DIM // 2
    inv_freq = ROPE_BASE ** (-jnp.arange(0, half, 2, dtype=F32) / half)
    ang = jnp.concatenate([row[:, None] * inv_freq[None], col[:, None] * inv_freq[None]], axis=-1)
    cos = jnp.repeat(jnp.cos(ang), 2, axis=-1)
    sin = jnp.repeat(jnp.sin(ang), 2, axis=-1) * jnp.tile(jnp.array([-1.0, 1.0], F32), half)
    return jnp.tile(cos, (1, PAIR)), jnp.tile(sin, (1, PAIR))


def _pair_gqa_heads(w, axis):
    group = GQA_HEADS // GQA_KV_HEADS
    shape = w.shape
    w = w.reshape(shape[:axis] + (GQA_KV_HEADS, group, HEAD_DIM) + shape[axis + 1:])
    return jnp.swapaxes(w, axis, axis + 1).reshape(shape)


def kernel(x, c, w_ada, b_ada, norm1_g, w_in, q_norm_g, k_norm_g, na_rpb, w_branch_na, w_branch_gqa,
           w_out, norm2_g, w_router, w_exp_gate, w_exp_up, w_exp_down, final_g):
    batch, seq, d = x.shape
    depth = w_ada.shape[0]
    n_exp = w_router.shape[2]
    cap = EC_CAPACITY_FACTOR * seq // n_exp
    na_w = NA_HEADS * HEAD_DIM
    gq_w = GQA_HEADS * HEAD_DIM
    kv_w = GQA_KV_HEADS * HEAD_DIM
    assert kv_w == LANES and GQA_HEADS // GQA_KV_HEADS == N_PAIRS and seq % LANES == 0

    cos_t, sin_t = _rope_tables(seq)
    x2 = x.reshape(batch * seq, d)
    c_pad = jnp.zeros((SUBLANES, d), F32).at[:batch].set(c)

    for l in range(depth):
        mod = _mod(c_pad, w_ada[l], b_ada[l][None, :])[:batch]
        shift1, scale1, gate1, shift2, scale2, gate2 = [m[:, None, :] for m in jnp.split(mod, 6, axis=-1)]

        w = w_in[l]
        q0 = 3 * na_w
        w_qkv = jnp.concatenate([w[:, :q0], _pair_gqa_heads(w[:, q0:q0 + gq_w], 1),
                                 w[:, q0 + gq_w:q0 + gq_w + kv_w]], axis=1).astype(BF16)
        w_vt = w[:, q0 + gq_w + kv_w:q0 + gq_w + 2 * kv_w].T.astype(BF16)
        w_gate = w[:, q0 + gq_w + 2 * kv_w:].astype(BF16)
        qg = jnp.tile(q_norm_g[l], PAIR)[None, :]
        kg = jnp.tile(k_norm_g[l], PAIR)[None, :]
        g1 = norm1_g[l][None, :]
        g2 = norm2_g[l][None, :]

        qna, kna, vna, qgq, kgq, vgqt = _proj(x2, shift1, scale1, g1, w_qkv, w_vt, cos_t, sin_t, qg, kg,
                                              batch, seq, tm=512)
        yna = _na(qna, kna, vna, _na_bias_table(na_rpb[l]))
        ygq = _gqa(qgq, kgq, vgqt, tq=256, kc=min(512, seq))

        x1, h2, aff = _merge(x2, (shift1, scale1, gate1, shift2, scale2), g1, g2, w_gate, yna, ygq,
                             w_branch_na[l].astype(BF16), _pair_gqa_heads(w_branch_gqa[l], 0).astype(BF16),
                             w_out[l].astype(BF16), w_router[l].T, batch, seq, tm=512)

        idx, gates = _topk(aff.reshape(batch, n_exp, seq // LANES, LANES), cap)
        idx3 = idx.reshape(batch * n_exp, 1, cap)
        gbits3 = lax.bitcast_convert_type(gates, I32).reshape(batch * n_exp, 1, cap)
        y = _ffn(idx3, h2, w_exp_gate[l].astype(BF16), w_exp_up[l].astype(BF16), w_exp_down[l].astype(BF16),
                 batch, seq)
        x2 = _combine(idx3, gbits3, y, gate2, x1.reshape(batch, seq, d), batch, seq,
                      tc=min(256, cap)).reshape(batch * seq, d)

    return _final(x2, final_g[None, :], tm=1024).reshape(batch, seq, d)
```
